```python
import math
import numpy as np
import jax
import jax.numpy as jnp
from jax import lax

D_MODEL = 1024
BATCH = 8
SEQ = 8192
DEPTH = 2

D_FF = 2816
CONV_K = 4
CHUNK = 64
EPS = 1e-6
GDN_HEADS = 8
GDN_HEAD_DIM = 128
GDN_W = GDN_HEADS * GDN_HEAD_DIM
SSM_HEADS = 16
SSM_HEAD_DIM = 64
SSM_GROUPS = 2
SSM_STATE = 128
SSM_W = SSM_HEADS * SSM_HEAD_DIM
SSM_BC_W = 2 * SSM_GROUPS * SSM_STATE
MIX_W = GDN_W + SSM_W
IN_W = 4 * GDN_W + 2 * GDN_HEADS + 2 * SSM_W + SSM_BC_W + SSM_HEADS
DT_MIN = 0.001
DT_MAX = 0.1
A_MIN = 1.0
A_MAX = 16.0

kernel_name = "hybrid_gdn_mamba2_macaron"


def rms_norm(x, w):
    xf = x.astype(jnp.float32)
    y = xf * lax.rsqrt(jnp.mean(xf * xf, axis=-1, keepdims=True) + EPS)
    return (y * w.astype(jnp.float32)).astype(x.dtype)


def l2_normalize(x):
    return x * lax.rsqrt(jnp.sum(x * x, axis=-1, keepdims=True) + EPS)


def swiglu(h, w_gate, w_up, w_down):
    a = jnp.einsum("bld,df->blf", h, w_gate)
    b = jnp.einsum("bld,df->blf", h, w_up)
    return jnp.einsum("blf,fd->bld", jax.nn.silu(a) * b, w_down)


def causal_dwconv(x, w):
    ch = x.shape[-1]
    return lax.conv_general_dilated(
        x, w[:, None, :].astype(x.dtype), window_strides=(1,),
        padding=[(CONV_K - 1, 0)], dimension_numbers=("NWC", "WIO", "NWC"),
        feature_group_count=ch)


def gated_delta_rule(q, k, v, g, beta):
    bsz, seqlen, nh, dk = q.shape
    dv = v.shape[-1]
    nc = seqlen // CHUNK

    def to_chunks(t):
        t = t.reshape(bsz, nc, CHUNK, nh, *t.shape[3:])
        return jnp.moveaxis(t, 3, 1)

    q = to_chunks(l2_normalize(q) * (dk ** -0.5))
    k = to_chunks(l2_normalize(k))
    v = to_chunks(v)
    beta = to_chunks(beta)
    gc = jnp.cumsum(to_chunks(g), axis=-1)
    incl = jnp.tril(jnp.ones((CHUNK, CHUNK), bool))
    strict = jnp.tril(jnp.ones((CHUNK, CHUNK), bool), -1)
    decay = jnp.exp(jnp.where(incl, gc[..., :, None] - gc[..., None, :], -jnp.inf))
    kk = jnp.einsum("bhcld,bhcsd->bhcls", k, k)
    a_low = jnp.where(strict, beta[..., :, None] * kk * decay, 0.0)
    eye = jnp.eye(CHUNK, dtype=q.dtype)
    t_inv = lax.linalg.triangular_solve(
        eye + a_low, jnp.broadcast_to(eye, a_low.shape),
        left_side=True, lower=True, unit_diagonal=True)
    u = jnp.einsum("bhcls,bhcsd->bhcld", t_inv, v * beta[..., None])
    w = jnp.einsum("bhcls,bhcsd->bhcld", t_inv, k * (beta * jnp.exp(gc))[..., None])
    qk = jnp.einsum("bhcld,bhcsd->bhcls", q, k) * decay
    q_dec = q * jnp.exp(gc)[..., None]
    k_dec = k * jnp.exp(gc[..., -1:] - gc)[..., None]
    g_tot = jnp.exp(gc[..., -1])
    xs = tuple(jnp.moveaxis(t, 2, 0) for t in (qk, u, w, q_dec, k_dec, g_tot))

    def step(state, inp):
        qk_c, u_c, w_c, qd_c, kd_c, gt_c = inp
        v_new = u_c - jnp.einsum("bhld,bhde->bhle", w_c, state)
        o = (jnp.einsum("bhld,bhde->bhle", qd_c, state)
             + jnp.einsum("bhls,bhse->bhle", qk_c, v_new))
        state = state * gt_c[..., None, None] + jnp.einsum("bhld,bhle->bhde", kd_c, v_new)
        return state, o

    s0 = jnp.zeros((bsz, nh, dk, dv), jnp.float32)
    _, o = lax.scan(step, s0, xs)
    o = jnp.transpose(o, (1, 0, 3, 2, 4))
    return o.reshape(bsz, seqlen, nh, dv)


def ssd_scan(x, dt, a_neg, b_in, c_in):
    bsz, seqlen, nh, p = x.shape
    ng, n = b_in.shape[2], b_in.shape[3]
    nj = nh // ng
    nc = seqlen // CHUNK
    a = (dt * a_neg).reshape(bsz, nc, CHUNK, ng, nj)
    xdt = (x * dt[..., None]).reshape(bsz, nc, CHUNK, ng, nj, p)
    bc = b_in.reshape(bsz, nc, CHUNK, ng, n)
    cc = c_in.reshape(bsz, nc, CHUNK, ng, n)
    acs = jnp.cumsum(a, axis=2)
    incl = jnp.tril(jnp.ones((CHUNK, CHUNK), bool))[:, :, None, None]
    seg = jnp.exp(jnp.where(incl, acs[:, :, :, None] - acs[:, :, None, :], -jnp.inf))
    cb = jnp.einsum("bclgn,bcsgn->bclsg", cc, bc)
    y_diag = jnp.einsum("bclsgj,bcsgjp->bclgjp", cb[..., None] * seg, xdt)
    x_to_end = xdt * jnp.exp(acs[:, :, -1:] - acs)[..., None]
    xs = tuple(jnp.moveaxis(t, 1, 0) for t in
               (bc, cc, x_to_end, jnp.exp(acs), jnp.exp(acs[:, :, -1])))

    def step(state, inp):
        b_c, c_c, xe_c, din_c, dtot_c = inp
        y_off = jnp.einsum("blgn,bgjpn->blgjp", c_c, state) * din_c[..., None]
        state = state * dtot_c[..., None, None] + jnp.einsum("blgn,blgjp->bgjpn", b_c, xe_c)
        return state, y_off

    s0 = jnp.zeros((bsz, ng, nj, p, n), jnp.float32)
    _, y_off = lax.scan(step, s0, xs)
    y = y_diag + jnp.moveaxis(y_off, 0, 1)
    return y.reshape(bsz, seqlen, nh, p)


def hybrid_mixer(h, w_in, gdn_conv_w, gdn_a_log, gdn_dt_bias, gdn_norm_w,
                 ssm_conv_w, ssm_conv_b, ssm_a_log, ssm_dt_bias, ssm_d, ssm_norm_w, w_out):
    bsz, seqlen, _ = h.shape
    proj = jnp.einsum("bld,de->ble", h, w_in).astype(jnp.float32)
    sizes = (3 * GDN_W, GDN_W, GDN_HEADS, GDN_HEADS, SSM_W, SSM_W + SSM_BC_W, SSM_HEADS)
    offs = np.cumsum(sizes)[:-1].tolist()
    gdn_qkv, gdn_z, gdn_b, gdn_a, ssm_z, ssm_xbc, ssm_dt = jnp.split(proj, offs, axis=-1)

    qkv = jax.nn.silu(causal_dwconv(gdn_qkv, gdn_conv_w))
    q, k, v = [t.reshape(bsz, seqlen, GDN_HEADS, GDN_HEAD_DIM) for t in jnp.split(qkv, 3, axis=-1)]
    beta = jax.nn.sigmoid(gdn_b)
    g = -jnp.exp(gdn_a_log.astype(jnp.float32)) * jax.nn.softplus(gdn_a + gdn_dt_bias)
    o = gated_delta_rule(q, k, v, g, beta)
    o = rms_norm(o, gdn_norm_w) * jax.nn.silu(gdn_z.reshape(bsz, seqlen, GDN_HEADS, GDN_HEAD_DIM))
    o_gdn = o.reshape(bsz, seqlen, GDN_W)

    xbc = jax.nn.silu(causal_dwconv(ssm_xbc, ssm_conv_w) + ssm_conv_b)
    xs, bs, cs = jnp.split(xbc, [SSM_W, SSM_W + SSM_GROUPS * SSM_STATE], axis=-1)
    xs = xs.reshape(bsz, seqlen, SSM_HEADS, SSM_HEAD_DIM)
    dt = jax.nn.softplus(ssm_dt + ssm_dt_bias)
    a_neg = -jnp.exp(ssm_a_log.astype(jnp.float32))
    y = ssd_scan(xs, dt, a_neg,
                 bs.reshape(bsz, seqlen, SSM_GROUPS, SSM_STATE),
                 cs.reshape(bsz, seqlen, SSM_GROUPS, SSM_STATE))
    y = y + xs * ssm_d[:, None]
    y = y.reshape(bsz, seqlen, SSM_W) * jax.nn.silu(ssm_z)
    y = rms_norm(y.reshape(bsz, seqlen, SSM_GROUPS, SSM_W // SSM_GROUPS),
                 ssm_norm_w.reshape(SSM_GROUPS, SSM_W // SSM_GROUPS))
    o_ssm = y.reshape(bsz, seqlen, SSM_W)

    mixed = jnp.concatenate([o_gdn, o_ssm], axis=-1).astype(h.dtype)
    return jnp.einsum("ble,ed->bld", mixed, w_out)


def setup_inputs(seed: int = 0) -> dict:
    key = jax.random.key(seed)
    ks = jax.random.split(key, 23)
    f32 = jnp.float32

    def dense(k, shape, fan_in):
        return jax.random.normal(k, shape, f32) * (fan_in ** -0.5)

    def gain(k, shape):
        return 1.0 + 0.02 * jax.random.normal(k, shape, f32)

    def dt_bias(k, n):
        dt = jnp.exp(jax.random.uniform(k, (DEPTH, n), f32, math.log(DT_MIN), math.log(DT_MAX)))
        return dt + jnp.log(-jnp.expm1(-dt))

    def a_log(k, n):
        return jnp.log(jax.random.uniform(k, (DEPTH, n), f32, A_MIN, A_MAX))

    return {
        "x": jax.random.normal(ks[0], (BATCH, SEQ, D_MODEL), f32),
        "ffn1_norm": gain(ks[1], (DEPTH, D_MODEL)),
        "ffn1_w_gate": dense(ks[2], (DEPTH, D_MODEL, D_FF), D_MODEL),
        "ffn1_w_up": dense(ks[3], (DEPTH, D_MODEL, D_FF), D_MODEL),
        "ffn1_w_down": dense(ks[4], (DEPTH, D_FF, D_MODEL), D_FF),
        "mix_norm": gain(ks[5], (DEPTH, D_MODEL)),
        "w_in": dense(ks[6], (DEPTH, D_MODEL, IN_W), D_MODEL),
        "gdn_conv_w": dense(ks[7], (DEPTH, CONV_K, 3 * GDN_W), CONV_K),
        "gdn_a_log": a_log(ks[8], GDN_HEADS),
        "gdn_dt_bias": dt_bias(ks[9], GDN_HEADS),
        "gdn_norm_w": gain(ks[10], (DEPTH, GDN_HEAD_DIM)),
        "ssm_conv_w": dense(ks[11], (DEPTH, CONV_K, SSM_W + SSM_BC_W), CONV_K),
        "ssm_conv_b": 0.02 * jax.random.normal(ks[12], (DEPTH, SSM_W + SSM_BC_W), f32),
        "ssm_a_log": a_log(ks[13], SSM_HEADS),
        "ssm_dt_bias": dt_bias(ks[14], SSM_HEADS),
        "ssm_d": 1.0 + 0.1 * jax.random.normal(ks[15], (DEPTH, SSM_HEADS), f32),
        "ssm_norm_w": gain(ks[16], (DEPTH, SSM_W)),
        "w_out": dense(ks[17], (DEPTH, MIX_W, D_MODEL), MIX_W),
        "ffn2_norm": gain(ks[18], (DEPTH, D_MODEL)),
        "ffn2_w_gate": dense(ks[19], (DEPTH, D_MODEL, D_FF), D_MODEL),
        "ffn2_w_up": dense(ks[20], (DEPTH, D_MODEL, D_FF), D_MODEL),
        "ffn2_w_down": dense(ks[21], (DEPTH, D_FF, D_MODEL), D_FF),
        "final_norm": gain(ks[22], (D_MODEL,)),
    }


def reference(x, ffn1_norm, ffn1_w_gate, ffn1_w_up, ffn1_w_down, mix_norm, w_in,
              gdn_conv_w, gdn_a_log, gdn_dt_bias, gdn_norm_w,
              ssm_conv_w, ssm_conv_b, ssm_a_log, ssm_dt_bias, ssm_d, ssm_norm_w,
              w_out, ffn2_norm, ffn2_w_gate, ffn2_w_up, ffn2_w_down, final_norm):
    for i in range(DEPTH):
        x = x + 0.5 * swiglu(rms_norm(x, ffn1_norm[i]), ffn1_w_gate[i], ffn1_w_up[i], ffn1_w_down[i])
        x = x + hybrid_mixer(rms_norm(x, mix_norm[i]), w_in[i],
                             gdn_conv_w[i], gdn_a_log[i], gdn_dt_bias[i], gdn_norm_w[i],
                             ssm_conv_w[i], ssm_conv_b[i], ssm_a_log[i], ssm_dt_bias[i],
                             ssm_d[i], ssm_norm_w[i], w_out[i])
        x = x + 0.5 * swiglu(rms_norm(x, ffn2_norm[i]), ffn2_w_gate[i], ffn2_w_up[i], ffn2_w_down[i])
    return rms_norm(x, final_norm)
```

```python
import functools

import jax
import jax.numpy as jnp
import numpy as np
from jax import lax
from jax.experimental import pallas as pl
from jax.experimental.pallas import tpu as pltpu

F32 = jnp.float32
BF16 = jnp.bfloat16

D_MODEL = 1024
D_FF = 2816
CONV_K = 4
CHUNK = 64
EPS = 1e-6
GDN_HEADS = 8
GDN_HEAD_DIM = 128
GDN_W = GDN_HEADS * GDN_HEAD_DIM
SSM_HEADS = 16
SSM_HEAD_DIM = 64
SSM_GROUPS = 2
SSM_STATE = 128
SSM_W = SSM_HEADS * SSM_HEAD_DIM
SSM_GROUP_W = SSM_W // SSM_GROUPS
SSM_BC_W = 2 * SSM_GROUPS * SSM_STATE

LANES = 128
CONV_W = 3 * GDN_W + SSM_W + SSM_BC_W
GATE_W = GDN_W + SSM_W
SMALL_W = LANES
IN_W_PADDED = CONV_W + GATE_W + SMALL_W
XBC_OFF = 3 * GDN_W
BETA_LANE = 0
GDN_A_LANE = GDN_HEADS
SSM_DT_LANE = 2 * GDN_HEADS
TAIL_ROWS = 8

FFN_TILE = 512
MIX_BLOCK = 256
VMEM_LIMIT_BYTES = 56 * 1024 * 1024

HI = lax.Precision.HIGHEST


def _rms(x, w):
    ms = jnp.mean(x * x, axis=-1, keepdims=True)
    return x * lax.rsqrt(ms + EPS) * w


def _silu(x):
    return x * jax.nn.sigmoid(x)


def _softplus(x):
    return jnp.maximum(x, 0.0) + jnp.log1p(jnp.exp(-jnp.abs(x)))


def _dot(a, b):
    return jnp.dot(a, b, preferred_element_type=F32)


def _dot_nt(a, b):
    return lax.dot_general(a, b, (((1,), (1,)), ((), ())), preferred_element_type=F32)


def _dot_tn(a, b):
    return lax.dot_general(a, b, (((0,), (0,)), ((), ())), preferred_element_type=F32)


def _dot_hi(a, b):
    return jnp.dot(a, b, precision=HI, preferred_element_type=F32)


def _ffn_kernel(x_ref, nw_ref, wg_ref, wu_ref, wd_ref, fw_ref, o_ref, *, final_norm):
    x = x_ref[...]
    h = _rms(x, nw_ref[...]).astype(BF16)
    a = _dot(h, wg_ref[...])
    b = _dot(h, wu_ref[...])
    g = (_silu(a) * b).astype(BF16)
    out = x + 0.5 * _dot(g, wd_ref[...])
    if final_norm:
        out = _rms(out, fw_ref[...])
    o_ref[...] = out


def _resident(shape):
    return pl.BlockSpec(shape, lambda *_: (0,) * len(shape), pipeline_mode=pl.Buffered(1))


def _ffn(x2d, norm_w, w_gate, w_up, w_down, final_w, final_norm):
    t, d = x2d.shape
    tm = min(FFN_TILE, t)
    assert t % tm == 0
    return pl.pallas_call(
        functools.partial(_ffn_kernel, final_norm=final_norm),
        grid=(t // tm,),
        in_specs=[
            pl.BlockSpec((tm, d), lambda i: (i, 0)),
            _resident((1, d)),
            _resident((d, D_FF)),
            _resident((d, D_FF)),
            _resident((D_FF, d)),
            _resident((1, d)),
        ],
        out_specs=pl.BlockSpec((tm, d), lambda i: (i, 0)),
        out_shape=jax.ShapeDtypeStruct((t, d), F32),
        compiler_params=pltpu.CompilerParams(
            dimension_semantics=("arbitrary",), vmem_limit_bytes=VMEM_LIMIT_BYTES),
    )(x2d, norm_w.reshape(1, d), w_gate.astype(BF16), w_up.astype(BF16), w_down.astype(BF16),
      final_w.reshape(1, d))


def _tri_inv(a):
    n = a.shape[0]
    r = lax.broadcasted_iota(jnp.int32, (n, n), 0)
    c = lax.broadcasted_iota(jnp.int32, (n, n), 1)
    t = jnp.where(r == c, 1.0, 0.0).astype(F32) - a
    p = _dot_hi(a, a)
    steps = int(np.log2(n)) - 1
    for i in range(steps):
        t = t + _dot_hi(t, p)
        if i + 1 < steps:
            p = _dot_hi(p, p)
    return t


def _mixer_kernel(x_ref, nw_ref, win_ref, gcw_ref, scw_ref, scb_ref, hp_ref, gnw_ref, dexp_ref,
                  snw_ref, sel_ref, wout_ref, o_ref,
                  pre, qkv, xbc, gates, sig_s, cs_s, din_s, toe_s, xdt_s, xe_s, dine_s, mix_s,
                  sg, ss, *, tb):
    nchunk = tb // CHUNK

    @pl.when(pl.program_id(1) == 0)
    def _():
        sg[...] = jnp.zeros_like(sg)
        ss[...] = jnp.zeros_like(ss)
        pre[0:TAIL_ROWS, :] = jnp.zeros((TAIL_ROWS, CONV_W), F32)

    x = x_ref[0]
    h = _rms(x, nw_ref[...]).astype(BF16)
    pre[TAIL_ROWS:TAIL_ROWS + tb, :] = _dot(h, win_ref[:, 0:CONV_W])
    gates[...] = _dot(h, win_ref[:, CONV_W:CONV_W + GATE_W])
    small = _dot(h, win_ref[:, CONV_W + GATE_W:IN_W_PADDED])

    sp = _softplus(small + hp_ref[0:1, :])
    sig_s[...] = jax.nn.sigmoid(small)
    logdec = -jnp.exp(hp_ref[1:2, :]) * sp

    r = lax.broadcasted_iota(jnp.int32, (tb, tb), 0)
    c = lax.broadcasted_iota(jnp.int32, (tb, tb), 1)
    same = (r // CHUNK) == (c // CHUNK)
    cs = _dot_hi(jnp.where(same & (c <= r), 1.0, 0.0).astype(F32), logdec)
    cs_last = _dot_hi(jnp.where(same, 1.0, 0.0).astype(F32), logdec)
    din = jnp.exp(cs)
    toe = jnp.exp(cs_last - cs)
    cs_s[...] = cs
    din_s[...] = din
    toe_s[...] = toe

    cw = 512
    for j in range(CONV_W // cw):
        lo = j * cw
        if lo < XBC_OFF:
            w_ref, wlo, bias = gcw_ref, lo, None
        else:
            w_ref, wlo, bias = scw_ref, lo - XBC_OFF, scb_ref[:, lo - XBC_OFF:lo - XBC_OFF + cw]
        acc = None
        for k in range(CONV_K):
            off = TAIL_ROWS - (CONV_K - 1) + k
            term = w_ref[k:k + 1, wlo:wlo + cw] * pre[off:off + tb, lo:lo + cw]
            acc = term if acc is None else acc + term
        if bias is not None:
            acc = acc + bias
        y = _silu(acc)
        if lo < XBC_OFF:
            qkv[:, lo:lo + cw] = y
        else:
            xbc[:, lo - XBC_OFF:lo - XBC_OFF + cw] = y
    pre[0:TAIL_ROWS, :] = pre[tb:tb + TAIL_ROWS, :]

    for hd in range(GDN_HEADS):
        for base, scale in ((0, GDN_HEAD_DIM ** -0.5), (GDN_W, 1.0)):
            sl = slice(base + hd * GDN_HEAD_DIM, base + (hd + 1) * GDN_HEAD_DIM)
            v = qkv[:, sl]
            qkv[:, sl] = v * (lax.rsqrt(jnp.sum(v * v, axis=-1, keepdims=True) + EPS) * scale)

    def expand(v):
        hi = v.astype(BF16)
        lo_ = (v - hi.astype(F32)).astype(BF16)
        return _dot(jnp.concatenate([hi, lo_], axis=1), sel_ref[...])

    xdt = xbc[:, 0:SSM_W] * expand(sp)
    xdt_s[...] = xdt
    xe_s[...] = xdt * expand(toe)
    dine_s[...] = expand(din)

    li = lax.broadcasted_iota(jnp.int32, (CHUNK, CHUNK), 0)
    si = lax.broadcasted_iota(jnp.int32, (CHUNK, CHUNK), 1)
    incl = si <= li
    strict = si < li
    half = lax.broadcasted_iota(jnp.int32, (CHUNK, LANES), 1) // SSM_HEAD_DIM

    def chunk_body(ci, carry):
        r0 = pl.multiple_of(ci * CHUNK, CHUNK)
        rows = pl.ds(r0, CHUNK)
        last = pl.ds(r0 + CHUNK - 1, 1)
        cs_c = cs_s[rows, :]
        cs_t = cs_c.T
        din_c = din_s[rows, :]
        toe_c = toe_s[rows, :]
        sig_c = sig_s[rows, :]
        din_last = din_s[last, :]

        for hd in range(GDN_HEADS):
            sl = slice(hd * GDN_HEAD_DIM, (hd + 1) * GDN_HEAD_DIM)
            qn = qkv[rows, sl]
            kn = qkv[rows, GDN_W + hd * GDN_HEAD_DIM:GDN_W + (hd + 1) * GDN_HEAD_DIM]
            v = qkv[rows, 2 * GDN_W + hd * GDN_HEAD_DIM:2 * GDN_W + (hd + 1) * GDN_HEAD_DIM]
            gl = GDN_A_LANE + hd
            beta = sig_c[:, BETA_LANE + hd:BETA_LANE + hd + 1]
            eg = din_c[:, gl:gl + 1]
            te = toe_c[:, gl:gl + 1]
            gtot = din_last[:, gl:gl + 1]
            decay = jnp.exp(jnp.where(incl, cs_c[:, gl:gl + 1] - cs_t[gl:gl + 1, :], -jnp.inf))
            knb = kn.astype(BF16)
            a_low = jnp.where(strict, beta * _dot_nt(knb, knb) * decay, 0.0)
            t_inv = _tri_inv(a_low)
            rhs = jnp.concatenate([v * beta, kn * (beta * eg)], axis=1).astype(BF16)
            uw = _dot(t_inv.astype(BF16), rhs)
            u = uw[:, 0:GDN_HEAD_DIM]
            w = uw[:, GDN_HEAD_DIM:2 * GDN_HEAD_DIM]
            qk = _dot_nt(qn.astype(BF16), knb) * decay
            state = sg[hd]
            wq = _dot(jnp.concatenate([w, qn * eg], axis=0).astype(BF16), state.astype(BF16))
            v_new = (u - wq[0:CHUNK]).astype(BF16)
            o = wq[CHUNK:2 * CHUNK] + _dot(qk.astype(BF16), v_new)
            sg[hd] = state * gtot + _dot_tn((kn * te).astype(BF16), v_new)
            z = gates[rows, sl]
            mix_s[rows, sl] = (_rms(o, gnw_ref[...]) * _silu(z)).astype(BF16)

        for g in range(SSM_GROUPS):
            gsl = slice(g * SSM_GROUP_W, (g + 1) * SSM_GROUP_W)
            bg = xbc[rows, SSM_W + g * SSM_STATE:SSM_W + (g + 1) * SSM_STATE].astype(BF16)
            cg = xbc[rows, SSM_W + (SSM_GROUPS + g) * SSM_STATE:
                     SSM_W + (SSM_GROUPS + g + 1) * SSM_STATE].astype(BF16)
            cb = _dot_nt(cg, bg)
            state = ss[g]
            y = _dot(cg, state.astype(BF16)) * dine_s[rows, gsl]
            ss[g] = state * dine_s[last, gsl] + _dot_tn(bg, xe_s[rows, gsl].astype(BF16))
            ydiag = []
            for pp in range(SSM_GROUP_W // LANES):
                xpair = xdt_s[rows, g * SSM_GROUP_W + pp * LANES:g * SSM_GROUP_W + (pp + 1) * LANES]
                acc = None
                for tt in range(LANES // SSM_HEAD_DIM):
                    hl = SSM_DT_LANE + g * (SSM_HEADS // SSM_GROUPS) + pp * (LANES // SSM_HEAD_DIM) + tt
                    seg = jnp.exp(jnp.where(incl, cs_c[:, hl:hl + 1] - cs_t[hl:hl + 1, :], -jnp.inf))
                    m = (cb * seg).astype(BF16)
                    term = _dot(m, jnp.where(half == tt, xpair, 0.0).astype(BF16))
                    acc = term if acc is None else acc + term
                ydiag.append(acc)
            y = y + jnp.concatenate(ydiag, axis=1)
            y = y + xbc[rows, gsl] * dexp_ref[:, gsl]
            y = y * _silu(gates[rows, GDN_W + g * SSM_GROUP_W:GDN_W + (g + 1) * SSM_GROUP_W])
            mix_s[rows, GDN_W + g * SSM_GROUP_W:GDN_W + (g + 1) * SSM_GROUP_W] = (
                _rms(y, snw_ref[:, gsl]).astype(BF16))
        return carry

    lax.fori_loop(0, nchunk, chunk_body, 0)

    o_ref[0] = x + _dot(mix_s[...], wout_ref[...])


def _mixer(x, norm_w, w_in, gdn_conv_w, gdn_a_log, gdn_dt_bias, gdn_norm_w,
           ssm_conv_w, ssm_conv_b, ssm_a_log, ssm_dt_bias, ssm_d, ssm_norm_w, w_out):
    bsz, seqlen, d = x.shape
    tb = min(MIX_BLOCK, seqlen)
    assert seqlen % tb == 0 and tb % CHUNK == 0

    o_qkv, o_z, o_b, o_a = 0, 3 * GDN_W, 4 * GDN_W, 4 * GDN_W + GDN_HEADS
    o_sz = o_a + GDN_HEADS
    o_xbc = o_sz + SSM_W
    o_dt = o_xbc + SSM_W + SSM_BC_W
    pad = jnp.zeros((d, SMALL_W - 2 * GDN_HEADS - SSM_HEADS), w_in.dtype)
    w_in_r = jnp.concatenate([
        w_in[:, o_qkv:o_z], w_in[:, o_xbc:o_dt], w_in[:, o_z:o_b], w_in[:, o_sz:o_xbc],
        w_in[:, o_b:o_a], w_in[:, o_a:o_sz], w_in[:, o_dt:o_dt + SSM_HEADS], pad], axis=1).astype(BF16)

    head_params = jnp.zeros((8, LANES), F32)
    head_params = head_params.at[0, GDN_A_LANE:GDN_A_LANE + GDN_HEADS].set(gdn_dt_bias)
    head_params = head_params.at[0, SSM_DT_LANE:SSM_DT_LANE + SSM_HEADS].set(ssm_dt_bias)
    head_params = head_params.at[1, GDN_A_LANE:GDN_A_LANE + GDN_HEADS].set(gdn_a_log)
    head_params = head_params.at[1, SSM_DT_LANE:SSM_DT_LANE + SSM_HEADS].set(ssm_a_log)

    d_exp = jnp.repeat(ssm_d, SSM_HEAD_DIM).reshape(1, SSM_W)
    sel = np.zeros((LANES, SSM_W), np.float32)
    for hd in range(SSM_HEADS):
        sel[SSM_DT_LANE + hd, hd * SSM_HEAD_DIM:(hd + 1) * SSM_HEAD_DIM] = 1.0
    sel2 = jnp.asarray(np.concatenate([sel, sel], axis=0), BF16)

    scratch = [
        pltpu.VMEM((TAIL_ROWS + tb, CONV_W), F32),
        pltpu.VMEM((tb, 3 * GDN_W), F32),
        pltpu.VMEM((tb, SSM_W + SSM_BC_W), F32),
        pltpu.VMEM((tb, GATE_W), F32),
        pltpu.VMEM((tb, LANES), F32),
        pltpu.VMEM((tb, LANES), F32),
        pltpu.VMEM((tb, LANES), F32),
        pltpu.VMEM((tb, LANES), F32),
        pltpu.VMEM((tb, SSM_W), F32),
        pltpu.VMEM((tb, SSM_W), F32),
        pltpu.VMEM((tb, SSM_W), F32),
        pltpu.VMEM((tb, GATE_W), BF16),
        pltpu.VMEM((GDN_HEADS, GDN_HEAD_DIM, GDN_HEAD_DIM), F32),
        pltpu.VMEM((SSM_GROUPS, SSM_STATE, SSM_GROUP_W), F32),
    ]
    return pl.pallas_call(
        functools.partial(_mixer_kernel, tb=tb),
        grid=(bsz, seqlen // tb),
        in_specs=[
            pl.BlockSpec((1, tb, d), lambda b, j: (b, j, 0)),
            _resident((1, d)),
            _resident((d, IN_W_PADDED)),
            _resident((CONV_K, 3 * GDN_W)),
            _resident((CONV_K, SSM_W + SSM_BC_W)),
            _resident((1, SSM_W + SSM_BC_W)),
            _resident((8, LANES)),
            _resident((1, GDN_HEAD_DIM)),
            _resident((1, SSM_W)),
            _resident((1, SSM_W)),
            _resident((2 * LANES, SSM_W)),
            _resident((GATE_W, d)),
        ],
        out_specs=pl.BlockSpec((1, tb, d), lambda b, j: (b, j, 0)),
        out_shape=jax.ShapeDtypeStruct((bsz, seqlen, d), F32),
        scratch_shapes=scratch,
        compiler_params=pltpu.CompilerParams(
            dimension_semantics=("arbitrary", "arbitrary"), vmem_limit_bytes=VMEM_LIMIT_BYTES),
    )(x, norm_w.reshape(1, d), w_in_r, gdn_conv_w, ssm_conv_w, ssm_conv_b.reshape(1, -1), head_params,
      gdn_norm_w.reshape(1, -1), d_exp, ssm_norm_w.reshape(1, -1), sel2, w_out.astype(BF16))


def kernel(x, ffn1_norm, ffn1_w_gate, ffn1_w_up, ffn1_w_down, mix_norm, w_in, gdn_conv_w, gdn_a_log,
           gdn_dt_bias, gdn_norm_w, ssm_conv_w, ssm_conv_b, ssm_a_log, ssm_dt_bias, ssm_d, ssm_norm_w,
           w_out, ffn2_norm, ffn2_w_gate, ffn2_w_up, ffn2_w_down, final_norm):
    bsz, seqlen, d = x.shape
    depth = ffn1_norm.shape[0]
    for i in range(depth):
        x2 = _ffn(x.reshape(bsz * seqlen, d), ffn1_norm[i], ffn1_w_gate[i], ffn1_w_up[i],
                  ffn1_w_down[i], final_norm, False)
        x = _mixer(x2.reshape(bsz, seqlen, d), mix_norm[i], w_in[i], gdn_conv_w[i], gdn_a_log[i],
                   gdn_dt_bias[i], gdn_norm_w[i], ssm_conv_w[i], ssm_conv_b[i], ssm_a_log[i],
                   ssm_dt_bias[i], ssm_d[i], ssm_norm_w[i], w_out[i])
        x2 = _ffn(x.reshape(bsz * seqlen, d), ffn2_norm[i], ffn2_w_gate[i], ffn2_w_up[i],
                  ffn2_w_down[i], final_norm, i == depth - 1)
        x = x2.reshape(bsz, seqlen, d)
    return x
```

```python
import functools

import jax
import jax.numpy as jnp
import numpy as np
from jax import lax
from jax.experimental import pallas as pl
from jax.experimental.pallas import tpu as pltpu

F32 = jnp.float32
BF16 = jnp.bfloat16

D_MODEL = 1024
D_FF = 2816
CONV_K = 4
EPS = 1e-6
GDN_HEADS = 8
GDN_HEAD_DIM = 128
GDN_W = GDN_HEADS * GDN_HEAD_DIM
SSM_HEADS = 16
SSM_HEAD_DIM = 64
SSM_GROUPS = 2
SSM_STATE = 128
SSM_W = SSM_HEADS * SSM_HEAD_DIM
SSM_GROUP_W = SSM_W // SSM_GROUPS
SSM_BC_W = 2 * SSM_GROUPS * SSM_STATE

LANES = 128
CONV_W = 3 * GDN_W + SSM_W + SSM_BC_W
GATE_W = GDN_W + SSM_W
SMALL_W = LANES
IN_W_PADDED = CONV_W + GATE_W + SMALL_W
XBC_OFF = 3 * GDN_W
BETA_LANE = 0
GDN_A_LANE = GDN_HEADS
SSM_DT_LANE = 2 * GDN_HEADS
TAIL_ROWS = 8

FFN_TILE = 512
CONV_CHUNK = 512
MIX_BLOCK = 256
VMEM_LIMIT_BYTES = 56 * 1024 * 1024

HI = lax.Precision.HIGHEST


def _rms(x, w):
    ms = jnp.mean(x * x, axis=-1, keepdims=True)
    return x * lax.rsqrt(ms + EPS) * w


def _silu(x):
    return x * jax.nn.sigmoid(x)


def _softplus(x):
    return jnp.maximum(x, 0.0) + jnp.log1p(jnp.exp(-jnp.abs(x)))


def _dot(a, b):
    return jnp.dot(a, b, preferred_element_type=F32)


def _dot_nt(a, b):
    return lax.dot_general(a, b, (((1,), (1,)), ((), ())), preferred_element_type=F32)


def _dot_tn(a, b):
    return lax.dot_general(a, b, (((0,), (0,)), ((), ())), preferred_element_type=F32)


def _ffn_kernel(x_ref, nw_ref, wg_ref, wu_ref, wd_ref, fw_ref, o_ref, *, final_norm):
    x = x_ref[...]
    h = _rms(x, nw_ref[...]).astype(BF16)
    a = _dot(h, wg_ref[...])
    b = _dot(h, wu_ref[...])
    g = (_silu(a) * b).astype(BF16)
    out = x + 0.5 * _dot(g, wd_ref[...])
    if final_norm:
        out = _rms(out, fw_ref[...])
    o_ref[...] = out


def _resident(shape):
    return pl.BlockSpec(shape, lambda *_: (0,) * len(shape), pipeline_mode=pl.Buffered(1))


def _ffn(x2d, norm_w, w_gate, w_up, w_down, final_w, final_norm):
    t, d = x2d.shape
    tm = min(FFN_TILE, t)
    assert t % tm == 0
    return pl.pallas_call(
        functools.partial(_ffn_kernel, final_norm=final_norm),
        grid=(t // tm,),
        in_specs=[
            pl.BlockSpec((tm, d), lambda i: (i, 0)),
            _resident((1, d)),
            _resident((d, D_FF)),
            _resident((d, D_FF)),
            _resident((D_FF, d)),
            _resident((1, d)),
        ],
        out_specs=pl.BlockSpec((tm, d), lambda i: (i, 0)),
        out_shape=jax.ShapeDtypeStruct((t, d), F32),
        compiler_params=pltpu.CompilerParams(
            dimension_semantics=("arbitrary",), vmem_limit_bytes=VMEM_LIMIT_BYTES),
    )(x2d, norm_w.reshape(1, d), w_gate.astype(BF16), w_up.astype(BF16), w_down.astype(BF16),
      final_w.reshape(1, d))


def _block_constants(tb):
    r = np.arange(tb)[:, None]
    c = np.arange(tb)[None, :]
    eye = (r == c).astype(np.float32)
    tril = (c <= r).astype(np.float32)
    negm = np.where(c <= r, 0.0, -np.inf).astype(np.float32)
    levels = [((r // 2 == c // 2) & (r > c)).astype(np.float32)]
    b = 2
    while b < tb:
        levels.append(-((r // (2 * b) == c // (2 * b)) & (r // b != c // b) & (r > c)).astype(np.float32))
        b *= 2
    return eye, negm, tril, np.stack(levels)


def _mixer_kernel(x_ref, nw_ref, win_ref, gcw_ref, scw_ref, scb_ref, hp_ref, gnw_ref, dexp_ref,
                  snw_ref, sel_ref, eye_ref, negm_ref, tril_ref, lvl_ref, wout_ref, o_ref,
                  pre, tail, qkv, xbc, gates, xdt_s, xe_s, dine_s, mix_s,
                  a_s, t_s, qk_s, rhs_s, lhs_s, kd_s, u_s, oq_s, vn_s, sg, ss, *, tb):
    n_levels = lvl_ref.shape[0]

    @pl.when(pl.program_id(1) == 0)
    def _():
        sg[...] = jnp.zeros_like(sg)
        ss[...] = jnp.zeros_like(ss)
        tail[...] = jnp.zeros_like(tail)

    x = x_ref[0]
    h = _rms(x, nw_ref[...]).astype(BF16)

    for j in range(CONV_W // CONV_CHUNK):
        lo = j * CONV_CHUNK
        buf = pre.at[j % 2]
        buf[0:TAIL_ROWS, :] = tail[:, lo:lo + CONV_CHUNK]
        buf[TAIL_ROWS:TAIL_ROWS + tb, :] = _dot(h, win_ref[:, lo:lo + CONV_CHUNK])
        if lo < XBC_OFF:
            w_ref, wlo, bias = gcw_ref, lo, None
        else:
            w_ref, wlo = scw_ref, lo - XBC_OFF
            bias = scb_ref[:, wlo:wlo + CONV_CHUNK]
        acc = None
        for k in range(CONV_K):
            off = TAIL_ROWS - (CONV_K - 1) + k
            term = w_ref[k:k + 1, wlo:wlo + CONV_CHUNK] * buf[off:off + tb, :]
            acc = term if acc is None else acc + term
        if bias is not None:
            acc = acc + bias
        y = _silu(acc)
        if lo < XBC_OFF:
            qkv[:, lo:lo + CONV_CHUNK] = y
        else:
            xbc[:, wlo:wlo + CONV_CHUNK] = y
        tail[:, lo:lo + CONV_CHUNK] = buf[tb:tb + TAIL_ROWS, :]

    gates[...] = _dot(h, win_ref[:, CONV_W:CONV_W + GATE_W])
    small = _dot(h, win_ref[:, CONV_W + GATE_W:IN_W_PADDED])

    sp = _softplus(small + hp_ref[0:1, :])
    sig = jax.nn.sigmoid(small)
    logdec = -jnp.exp(hp_ref[1:2, :]) * sp

    cs = jnp.dot(tril_ref[...], logdec, precision=HI, preferred_element_type=F32)
    cs_t = cs.T
    din = jnp.exp(cs)
    toe = jnp.exp(cs[tb - 1:tb, :] - cs)
    din_last = din[tb - 1:tb, :]

    def expand(v):
        hi = v.astype(BF16)
        lo_ = (v - hi.astype(F32)).astype(BF16)
        return _dot(jnp.concatenate([hi, lo_], axis=1), sel_ref[...])

    xdt = xbc[:, 0:SSM_W] * expand(sp)
    xdt_s[...] = xdt
    xe_s[...] = xdt * expand(toe)
    dine_s[...] = expand(din)

    for hd in range(GDN_HEADS):
        sl = slice(hd * GDN_HEAD_DIM, (hd + 1) * GDN_HEAD_DIM)
        q = qkv[:, sl]
        k = qkv[:, GDN_W + hd * GDN_HEAD_DIM:GDN_W + (hd + 1) * GDN_HEAD_DIM]
        v = qkv[:, 2 * GDN_W + hd * GDN_HEAD_DIM:2 * GDN_W + (hd + 1) * GDN_HEAD_DIM]
        qn = q * (lax.rsqrt(jnp.sum(q * q, axis=-1, keepdims=True) + EPS) * GDN_HEAD_DIM ** -0.5)
        kn = k * lax.rsqrt(jnp.sum(k * k, axis=-1, keepdims=True) + EPS)
        gl = GDN_A_LANE + hd
        beta = sig[:, BETA_LANE + hd:BETA_LANE + hd + 1]
        eg = din[:, gl:gl + 1]
        decay = jnp.exp(cs[:, gl:gl + 1] - cs_t[gl:gl + 1, :] + negm_ref[...])
        knb = kn.astype(BF16)
        kq = _dot_nt(jnp.concatenate([knb, qn.astype(BF16)], axis=0), knb)
        a_b = (beta * kq[0:tb] * decay).astype(BF16)
        a_s[hd] = a_b
        t_s[hd] = (eye_ref[...] - (a_b * lvl_ref[0]).astype(F32)).astype(BF16)
        qk_s[hd] = (kq[tb:2 * tb] * decay).astype(BF16)
        rhs_s[hd] = jnp.concatenate([v * beta, kn * (beta * eg)], axis=1).astype(BF16)
        lhs_s[hd, tb:2 * tb, :] = (qn * eg).astype(BF16)
        kd_s[hd] = (kn * toe[:, gl:gl + 1]).astype(BF16)

    for lv in range(1, n_levels):
        for hd in range(GDN_HEADS):
            t_b = t_s[hd]
            xneg = _dot(a_s[hd] * lvl_ref[lv], t_b)
            t_s[hd] = _dot(t_b, (eye_ref[...] + xneg).astype(BF16)).astype(BF16)

    for hd in range(GDN_HEADS):
        uw = _dot(t_s[hd], rhs_s[hd])
        u_s[hd] = uw[:, 0:GDN_HEAD_DIM]
        lhs_s[hd, 0:tb, :] = uw[:, GDN_HEAD_DIM:2 * GDN_HEAD_DIM].astype(BF16)

    for hd in range(GDN_HEADS):
        wq = _dot(lhs_s[hd], sg[hd].astype(BF16))
        vn_s[hd] = (u_s[hd] - wq[0:tb]).astype(BF16)
        oq_s[hd] = wq[tb:2 * tb]

    for hd in range(GDN_HEADS):
        sl = slice(hd * GDN_HEAD_DIM, (hd + 1) * GDN_HEAD_DIM)
        gl = GDN_A_LANE + hd
        v_new = vn_s[hd]
        o = oq_s[hd] + _dot(qk_s[hd], v_new)
        sg[hd] = sg[hd] * din_last[:, gl:gl + 1] + _dot_tn(kd_s[hd], v_new)
        mix_s[:, sl] = (_rms(o, gnw_ref[...]) * _silu(gates[:, sl])).astype(BF16)

    half = lax.broadcasted_iota(jnp.int32, (tb, LANES), 1) // SSM_HEAD_DIM
    for g in range(SSM_GROUPS):
        gsl = slice(g * SSM_GROUP_W, (g + 1) * SSM_GROUP_W)
        bg = xbc[:, SSM_W + g * SSM_STATE:SSM_W + (g + 1) * SSM_STATE].astype(BF16)
        cg = xbc[:, SSM_W + (SSM_GROUPS + g) * SSM_STATE:
                 SSM_W + (SSM_GROUPS + g + 1) * SSM_STATE].astype(BF16)
        cb = _dot_nt(cg, bg)
        state = ss[g]
        y_off = _dot(cg, state.astype(BF16)) * dine_s[:, gsl]
        ss[g] = state * dine_s[tb - 1:tb, gsl] + _dot_tn(bg, xe_s[:, gsl].astype(BF16))
        for pp in range(SSM_GROUP_W // LANES):
            csl = slice(g * SSM_GROUP_W + pp * LANES, g * SSM_GROUP_W + (pp + 1) * LANES)
            xpair = xdt_s[:, csl]
            acc = y_off[:, pp * LANES:(pp + 1) * LANES]
            for tt in range(LANES // SSM_HEAD_DIM):
                hl = SSM_DT_LANE + g * (SSM_HEADS // SSM_GROUPS) + pp * (LANES // SSM_HEAD_DIM) + tt
                seg = jnp.exp(cs[:, hl:hl + 1] - cs_t[hl:hl + 1, :] + negm_ref[...])
                acc = acc + _dot((cb * seg).astype(BF16), jnp.where(half == tt, xpair, 0.0).astype(BF16))
            y = acc + xbc[:, csl] * dexp_ref[:, csl]
            gsl2 = slice(GDN_W + g * SSM_GROUP_W + pp * LANES, GDN_W + g * SSM_GROUP_W + (pp + 1) * LANES)
            xe_s[:, csl] = y * _silu(gates[:, gsl2])
        yg = xe_s[:, gsl]
        mix_s[:, GDN_W + g * SSM_GROUP_W:GDN_W + (g + 1) * SSM_GROUP_W] = (
            _rms(yg, snw_ref[:, gsl]).astype(BF16))

    o_ref[0] = x + _dot(mix_s[...], wout_ref[...])


def _mixer(x, norm_w, w_in, gdn_conv_w, gdn_a_log, gdn_dt_bias, gdn_norm_w,
           ssm_conv_w, ssm_conv_b, ssm_a_log, ssm_dt_bias, ssm_d, ssm_norm_w, w_out):
    bsz, seqlen, d = x.shape
    tb = min(MIX_BLOCK, seqlen)
    assert seqlen % tb == 0 and tb & (tb - 1) == 0

    o_qkv, o_z, o_b, o_a = 0, 3 * GDN_W, 4 * GDN_W, 4 * GDN_W + GDN_HEADS
    o_sz = o_a + GDN_HEADS
    o_xbc = o_sz + SSM_W
    o_dt = o_xbc + SSM_W + SSM_BC_W
    pad = jnp.zeros((d, SMALL_W - 2 * GDN_HEADS - SSM_HEADS), w_in.dtype)
    w_in_r = jnp.concatenate([
        w_in[:, o_qkv:o_z], w_in[:, o_xbc:o_dt], w_in[:, o_z:o_b], w_in[:, o_sz:o_xbc],
        w_in[:, o_b:o_a], w_in[:, o_a:o_sz], w_in[:, o_dt:o_dt + SSM_HEADS], pad], axis=1).astype(BF16)

    head_params = jnp.zeros((8, LANES), F32)
    head_params = head_params.at[0, GDN_A_LANE:GDN_A_LANE + GDN_HEADS].set(gdn_dt_bias)
    head_params = head_params.at[0, SSM_DT_LANE:SSM_DT_LANE + SSM_HEADS].set(ssm_dt_bias)
    head_params = head_params.at[1, GDN_A_LANE:GDN_A_LANE + GDN_HEADS].set(gdn_a_log)
    head_params = head_params.at[1, SSM_DT_LANE:SSM_DT_LANE + SSM_HEADS].set(ssm_a_log)

    d_exp = jnp.repeat(ssm_d, SSM_HEAD_DIM).reshape(1, SSM_W)
    sel = np.zeros((LANES, SSM_W), np.float32)
    for hd in range(SSM_HEADS):
        sel[SSM_DT_LANE + hd, hd * SSM_HEAD_DIM:(hd + 1) * SSM_HEAD_DIM] = 1.0
    sel2 = jnp.asarray(np.concatenate([sel, sel], axis=0), BF16)
    eye, negm, tril, levels = _block_constants(tb)
    n_levels = levels.shape[0]

    scratch = [
        pltpu.VMEM((2, TAIL_ROWS + tb, CONV_CHUNK), F32),
        pltpu.VMEM((TAIL_ROWS, CONV_W), F32),
        pltpu.VMEM((tb, 3 * GDN_W), F32),
        pltpu.VMEM((tb, SSM_W + SSM_BC_W), F32),
        pltpu.VMEM((tb, GATE_W), F32),
        pltpu.VMEM((tb, SSM_W), F32),
        pltpu.VMEM((tb, SSM_W), F32),
        pltpu.VMEM((tb, SSM_W), F32),
        pltpu.VMEM((tb, GATE_W), BF16),
        pltpu.VMEM((GDN_HEADS, tb, tb), BF16),
        pltpu.VMEM((GDN_HEADS, tb, tb), BF16),
        pltpu.VMEM((GDN_HEADS, tb, tb), BF16),
        pltpu.VMEM((GDN_HEADS, tb, 2 * GDN_HEAD_DIM), BF16),
        pltpu.VMEM((GDN_HEADS, 2 * tb, GDN_HEAD_DIM), BF16),
        pltpu.VMEM((GDN_HEADS, tb, GDN_HEAD_DIM), BF16),
        pltpu.VMEM((GDN_HEADS, tb, GDN_HEAD_DIM), F32),
        pltpu.VMEM((GDN_HEADS, tb, GDN_HEAD_DIM), F32),
        pltpu.VMEM((GDN_HEADS, tb, GDN_HEAD_DIM), BF16),
        pltpu.VMEM((GDN_HEADS, GDN_HEAD_DIM, GDN_HEAD_DIM), F32),
        pltpu.VMEM((SSM_GROUPS, SSM_STATE, SSM_GROUP_W), F32),
    ]
    return pl.pallas_call(
        functools.partial(_mixer_kernel, tb=tb),
        grid=(bsz, seqlen // tb),
        in_specs=[
            pl.BlockSpec((1, tb, d), lambda b, j: (b, j, 0)),
            _resident((1, d)),
            _resident((d, IN_W_PADDED)),
            _resident((CONV_K, 3 * GDN_W)),
            _resident((CONV_K, SSM_W + SSM_BC_W)),
            _resident((1, SSM_W + SSM_BC_W)),
            _resident((8, LANES)),
            _resident((1, GDN_HEAD_DIM)),
            _resident((1, SSM_W)),
            _resident((1, SSM_W)),
            _resident((2 * LANES, SSM_W)),
            _resident((tb, tb)),
            _resident((tb, tb)),
            _resident((tb, tb)),
            _resident((n_levels, tb, tb)),
            _resident((GATE_W, d)),
        ],
        out_specs=pl.BlockSpec((1, tb, d), lambda b, j: (b, j, 0)),
        out_shape=jax.ShapeDtypeStruct((bsz, seqlen, d), F32),
        scratch_shapes=scratch,
        compiler_params=pltpu.CompilerParams(
            dimension_semantics=("arbitrary", "arbitrary"), vmem_limit_bytes=VMEM_LIMIT_BYTES),
    )(x, norm_w.reshape(1, d), w_in_r, gdn_conv_w, ssm_conv_w, ssm_conv_b.reshape(1, -1), head_params,
      gdn_norm_w.reshape(1, -1), d_exp, ssm_norm_w.reshape(1, -1), sel2,
      jnp.asarray(eye), jnp.asarray(negm), jnp.asarray(tril), jnp.asarray(levels, BF16),
      w_out.astype(BF16))


def kernel(x, ffn1_norm, ffn1_w_gate, ffn1_w_up, ffn1_w_down, mix_norm, w_in, gdn_conv_w, gdn_a_log,
           gdn_dt_bias, gdn_norm_w, ssm_conv_w, ssm_conv_b, ssm_a_log, ssm_dt_bias, ssm_d, ssm_norm_w,
           w_out, ffn2_norm, ffn2_w_gate, ffn2_w_up, ffn2_w_down, final_norm):
    bsz, seqlen, d = x.shape
    depth = ffn1_norm.shape[0]
    for i in range(depth):
        x2 = _ffn(x.reshape(bsz * seqlen, d), ffn1_norm[i], ffn1_w_gate[i], ffn1_w_up[i],
                  ffn1_w_down[i], final_norm, False)
        x = _mixer(x2.reshape(bsz, seqlen, d), mix_norm[i], w_in[i], gdn_conv_w[i], gdn_a_log[i],
                   gdn_dt_bias[i], gdn_norm_w[i], ssm_conv_w[i], ssm_conv_b[i], ssm_a_log[i],
                   ssm_dt_bias[i], ssm_d[i], ssm_norm_w[i], w_out[i])
        x2 = _ffn(x.reshape(bsz * seqlen, d), ffn2_norm[i], ffn2_w_gate[i], ffn2_w_up[i],
                  ffn2_w_down[i], final_norm, i == depth - 1)
        x = x2.reshape(bsz, seqlen, d)
    return x
```

```python
import functools

import jax
import jax.numpy as jnp
import numpy as np
from jax import lax
from jax.experimental import pallas as pl
from jax.experimental.pallas import tpu as pltpu

F32 = jnp.float32
BF16 = jnp.bfloat16

D_MODEL = 1024
D_FF = 2816
CONV_K = 4
EPS = 1e-6
GDN_HEADS = 8
GDN_HEAD_DIM = 128
GDN_W = GDN_HEADS * GDN_HEAD_DIM
SSM_HEADS = 16
SSM_HEAD_DIM = 64
SSM_GROUPS = 2
SSM_STATE = 128
SSM_W = SSM_HEADS * SSM_HEAD_DIM
SSM_GROUP_W = SSM_W // SSM_GROUPS
SSM_BC_W = 2 * SSM_GROUPS * SSM_STATE

LANES = 128
CONV_W = 3 * GDN_W + SSM_W + SSM_BC_W
GATE_W = GDN_W + SSM_W
SMALL_W = LANES
IN_W_PADDED = CONV_W + GATE_W + SMALL_W
XBC_OFF = 3 * GDN_W
BETA_LANE = 0
GDN_A_LANE = GDN_HEADS
SSM_DT_LANE = 2 * GDN_HEADS
TAIL_ROWS = 8

FFN_TILE = 512
CONV_CHUNK = 512
PRE_BUFFERS = 2
MIX_BLOCK = 256
VMEM_LIMIT_BYTES = 56 * 1024 * 1024

LOG2E = 1.4426950408889634


def _rms(x, w):
    ms = jnp.mean(x * x, axis=-1, keepdims=True)
    return x * lax.rsqrt(ms + EPS) * w


def _silu(x):
    return x * jax.nn.sigmoid(x)


def _softplus(x):
    return jnp.maximum(x, 0.0) + jnp.log1p(jnp.exp(-jnp.abs(x)))


def _dot(a, b):
    return jnp.dot(a, b, preferred_element_type=F32)


def _dot_nt(a, b):
    return lax.dot_general(a, b, (((1,), (1,)), ((), ())), preferred_element_type=F32)


def _dot_tn(a, b):
    return lax.dot_general(a, b, (((0,), (0,)), ((), ())), preferred_element_type=F32)


def _ffn_kernel(x_ref, nw_ref, wg_ref, wu_ref, wd_ref, fw_ref, o_ref, *, final_norm):
    x = x_ref[...]
    h = _rms(x, nw_ref[...]).astype(BF16)
    a = _dot(h, wg_ref[...])
    b = _dot(h, wu_ref[...])
    g = (_silu(a) * b).astype(BF16)
    out = x + 0.5 * _dot(g, wd_ref[...])
    if final_norm:
        out = _rms(out, fw_ref[...])
    o_ref[...] = out


def _resident(shape):
    return pl.BlockSpec(shape, lambda *_: (0,) * len(shape), pipeline_mode=pl.Buffered(1))


def _ffn(x2d, norm_w, w_gate, w_up, w_down, final_w, final_norm):
    t, d = x2d.shape
    tm = min(FFN_TILE, t)
    assert t % tm == 0
    return pl.pallas_call(
        functools.partial(_ffn_kernel, final_norm=final_norm),
        grid=(t // tm,),
        in_specs=[
            pl.BlockSpec((tm, d), lambda i: (i, 0)),
            _resident((1, d)),
            _resident((d, D_FF)),
            _resident((d, D_FF)),
            _resident((D_FF, d)),
            _resident((1, d)),
        ],
        out_specs=pl.BlockSpec((tm, d), lambda i: (i, 0)),
        out_shape=jax.ShapeDtypeStruct((t, d), F32),
        compiler_params=pltpu.CompilerParams(
            dimension_semantics=("arbitrary",), vmem_limit_bytes=VMEM_LIMIT_BYTES),
    )(x2d, norm_w.reshape(1, d), w_gate.astype(BF16), w_up.astype(BF16), w_down.astype(BF16),
      final_w.reshape(1, d))


def _block_constants(tb):
    r = np.arange(tb)[:, None]
    c = np.arange(tb)[None, :]
    eye = (r == c).astype(np.float32)
    tril = (c <= r).astype(np.float32)
    negm = np.where(c <= r, 0.0, -np.inf).astype(np.float32)
    levels = [((r // 2 == c // 2) & (r > c)).astype(np.float32)]
    b = 2
    while b < tb:
        levels.append(-((r // (2 * b) == c // (2 * b)) & (r // b != c // b) & (r > c)).astype(np.float32))
        b *= 2
    return eye, negm, tril, np.stack(levels)


def _mixer_kernel(x_ref, nw_ref, win_ref, gcw_ref, scw_ref, scb_ref, hp_ref, gnw_ref, dexp_ref,
                  snw_ref, sel_ref, eye_ref, negm_ref, tril_ref, lvl_ref, wout_ref, o_ref,
                  pre, tail, qkv, xbc, gates, xdt_s, xe_s, dine_s, cb_s, mix_s,
                  a_s, t_s, qk_s, rhs_s, lhs_s, kd_s, u_s, oq_s, vn_s, sg, ss, *, tb):
    n_levels = lvl_ref.shape[0]
    n_gate_chunks = GATE_W // CONV_CHUNK
    n_conv_chunks = CONV_W // CONV_CHUNK

    @pl.when(pl.program_id(1) == 0)
    def _():
        sg[...] = jnp.zeros_like(sg)
        ss[...] = jnp.zeros_like(ss)
        tail[...] = jnp.zeros_like(tail)

    x = x_ref[0]
    h = _rms(x, nw_ref[...]).astype(BF16)

    small = _dot(h, win_ref[:, CONV_W + GATE_W:IN_W_PADDED])
    sp = _softplus(small + hp_ref[0:1, :])
    sig = jax.nn.sigmoid(small)
    logdec = -jnp.exp(hp_ref[1:2, :]) * sp

    ld_hi = logdec.astype(BF16)
    ld_r = logdec - ld_hi.astype(F32)
    ld_mid = ld_r.astype(BF16)
    ld_lo = (ld_r - ld_mid.astype(F32)).astype(BF16)
    cs3 = _dot(tril_ref[...], jnp.concatenate([ld_hi, ld_mid, ld_lo], axis=1))
    cs = (cs3[:, 0:LANES] + cs3[:, LANES:2 * LANES]) + cs3[:, 2 * LANES:3 * LANES]
    din = jnp.exp(cs)
    toe = jnp.exp(cs[tb - 1:tb, :] - cs)
    din_last = din[tb - 1:tb, :]
    cs_l2 = cs * LOG2E
    cs_l2t = cs_l2.T

    def expand(v):
        hi = v.astype(BF16)
        lo_ = (v - hi.astype(F32)).astype(BF16)
        return _dot(jnp.concatenate([hi, lo_], axis=1), sel_ref[...])

    xdt_s[...] = expand(sp)
    xe_s[...] = expand(toe)
    dine_s[...] = expand(din)

    gates_done = 0
    for j in range(n_conv_chunks):
        lo = j * CONV_CHUNK
        buf = pre.at[j % PRE_BUFFERS]
        buf[0:TAIL_ROWS, :] = tail[:, lo:lo + CONV_CHUNK]
        buf[TAIL_ROWS:TAIL_ROWS + tb, :] = _dot(h, win_ref[:, lo:lo + CONV_CHUNK])
        if j % 2 == 1 and gates_done < n_gate_chunks:
            glo = gates_done * CONV_CHUNK
            gates_done += 1
            gates[:, glo:glo + CONV_CHUNK] = _dot(h, win_ref[:, CONV_W + glo:CONV_W + glo + CONV_CHUNK])
        if lo < XBC_OFF:
            w_ref, wlo, bias = gcw_ref, lo, None
        else:
            w_ref, wlo = scw_ref, lo - XBC_OFF
            bias = scb_ref[:, wlo:wlo + CONV_CHUNK]
        w0, w1, w2, w3 = (w_ref[k:k + 1, wlo:wlo + CONV_CHUNK] for k in range(CONV_K))
        p_all = buf[...]
        p_m2 = pltpu.roll(p_all, 2, 0)
        odd = pltpu.roll(w2 * p_all + w0 * p_m2, 1, 0)
        acc = (w3 * p_all[TAIL_ROWS:] + w1 * p_m2[TAIL_ROWS:]) + odd[TAIL_ROWS:]
        if bias is not None:
            acc = acc + bias
        y = _silu(acc)
        if lo < XBC_OFF:
            qkv[:, lo:lo + CONV_CHUNK] = y
        else:
            xbc[:, wlo:wlo + CONV_CHUNK] = y
        tail[:, lo:lo + CONV_CHUNK] = buf[tb:tb + TAIL_ROWS, :]
    assert gates_done == n_gate_chunks

    xdt = xbc[:, 0:SSM_W] * xdt_s[...]
    xdt_s[...] = xdt
    xe_s[...] = xdt * xe_s[...]
    for g in range(SSM_GROUPS):
        gsl = slice(g * SSM_GROUP_W, (g + 1) * SSM_GROUP_W)
        bg = xbc[:, SSM_W + g * SSM_STATE:SSM_W + (g + 1) * SSM_STATE].astype(BF16)
        cg = xbc[:, SSM_W + (SSM_GROUPS + g) * SSM_STATE:
                 SSM_W + (SSM_GROUPS + g + 1) * SSM_STATE].astype(BF16)
        cb_s[g] = _dot_nt(cg, bg)
        state = ss[g]
        ss[g] = state * dine_s[tb - 1:tb, gsl] + _dot_tn(bg, xe_s[:, gsl].astype(BF16))
        dine_s[:, gsl] = _dot(cg, state.astype(BF16)) * dine_s[:, gsl]

    half = lax.broadcasted_iota(jnp.int32, (tb, LANES), 1) // SSM_HEAD_DIM
    pairs_per_group = SSM_GROUP_W // LANES

    def ssd_pair(pi):
        g, pp = divmod(pi, pairs_per_group)
        csl = slice(g * SSM_GROUP_W + pp * LANES, g * SSM_GROUP_W + (pp + 1) * LANES)
        xpair = xdt_s[:, csl]
        acc = dine_s[:, csl]
        for tt in range(LANES // SSM_HEAD_DIM):
            hl = SSM_DT_LANE + g * (SSM_HEADS // SSM_GROUPS) + pp * (LANES // SSM_HEAD_DIM) + tt
            seg = jnp.exp2(cs_l2[:, hl:hl + 1] - cs_l2t[hl:hl + 1, :] + negm_ref[...])
            acc = acc + _dot((cb_s[g] * seg).astype(BF16), jnp.where(half == tt, xpair, 0.0).astype(BF16))
        y = acc + xbc[:, csl] * dexp_ref[:, csl]
        gsl2 = slice(GDN_W + g * SSM_GROUP_W + pp * LANES, GDN_W + g * SSM_GROUP_W + (pp + 1) * LANES)
        xe_s[:, csl] = y * _silu(gates[:, gsl2])

    eye_b = eye_ref[...].astype(BF16)
    for hd in range(GDN_HEADS):
        sl = slice(hd * GDN_HEAD_DIM, (hd + 1) * GDN_HEAD_DIM)
        q = qkv[:, sl]
        k = qkv[:, GDN_W + hd * GDN_HEAD_DIM:GDN_W + (hd + 1) * GDN_HEAD_DIM]
        v = qkv[:, 2 * GDN_W + hd * GDN_HEAD_DIM:2 * GDN_W + (hd + 1) * GDN_HEAD_DIM]
        qn = q * (lax.rsqrt(jnp.sum(q * q, axis=-1, keepdims=True) + EPS) * GDN_HEAD_DIM ** -0.5)
        kn = k * lax.rsqrt(jnp.sum(k * k, axis=-1, keepdims=True) + EPS)
        gl = GDN_A_LANE + hd
        beta = sig[:, BETA_LANE + hd:BETA_LANE + hd + 1]
        eg = din[:, gl:gl + 1]
        decay = jnp.exp2(cs_l2[:, gl:gl + 1] - cs_l2t[gl:gl + 1, :] + negm_ref[...])
        knb = kn.astype(BF16)
        kq = _dot_nt(jnp.concatenate([knb, qn.astype(BF16)], axis=0), knb)
        a_b = (beta * kq[0:tb] * decay).astype(BF16)
        a_s[hd] = a_b
        t_s[hd] = eye_b - a_b * lvl_ref[0]
        qk_s[hd] = (kq[tb:2 * tb] * decay).astype(BF16)
        rhs_s[hd] = jnp.concatenate([v * beta, kn * (beta * eg)], axis=1).astype(BF16)
        lhs_s[hd, tb:2 * tb, :] = (qn * eg).astype(BF16)
        kd_s[hd] = (kn * toe[:, gl:gl + 1]).astype(BF16)

    n_pairs = SSM_GROUPS * pairs_per_group
    done = 0
    for lv in range(1, n_levels):
        for hd in range(GDN_HEADS):
            t_b = t_s[hd]
            xneg = _dot(a_s[hd] * lvl_ref[lv], t_b)
            t_s[hd] = t_b + _dot(t_b, xneg.astype(BF16)).astype(BF16)
        upto = (lv * n_pairs) // (n_levels - 1)
        for pi in range(done, upto):
            ssd_pair(pi)
        done = upto
    for pi in range(done, n_pairs):
        ssd_pair(pi)

    for hd in range(GDN_HEADS):
        uw = _dot(t_s[hd], rhs_s[hd])
        u_s[hd] = uw[:, 0:GDN_HEAD_DIM]
        lhs_s[hd, 0:tb, :] = uw[:, GDN_HEAD_DIM:2 * GDN_HEAD_DIM].astype(BF16)

    for hd in range(GDN_HEADS):
        wq = _dot(lhs_s[hd], sg[hd].astype(BF16))
        vn_s[hd] = (u_s[hd] - wq[0:tb]).astype(BF16)
        oq_s[hd] = wq[tb:2 * tb]

    for hd in range(GDN_HEADS):
        sl = slice(hd * GDN_HEAD_DIM, (hd + 1) * GDN_HEAD_DIM)
        gl = GDN_A_LANE + hd
        v_new = vn_s[hd]
        o = oq_s[hd] + _dot(qk_s[hd], v_new)
        sg[hd] = sg[hd] * din_last[:, gl:gl + 1] + _dot_tn(kd_s[hd], v_new)
        mix_s[:, sl] = (_rms(o, gnw_ref[...]) * _silu(gates[:, sl])).astype(BF16)

    for g in range(SSM_GROUPS):
        gsl = slice(g * SSM_GROUP_W, (g + 1) * SSM_GROUP_W)
        mix_s[:, GDN_W + g * SSM_GROUP_W:GDN_W + (g + 1) * SSM_GROUP_W] = (
            _rms(xe_s[:, gsl], snw_ref[:, gsl]).astype(BF16))

    o_ref[0] = x + _dot(mix_s[...], wout_ref[...])


def _mixer(x, norm_w, w_in, gdn_conv_w, gdn_a_log, gdn_dt_bias, gdn_norm_w,
           ssm_conv_w, ssm_conv_b, ssm_a_log, ssm_dt_bias, ssm_d, ssm_norm_w, w_out):
    bsz, seqlen, d = x.shape
    tb = min(MIX_BLOCK, seqlen)
    assert seqlen % tb == 0 and tb & (tb - 1) == 0

    o_qkv, o_z, o_b, o_a = 0, 3 * GDN_W, 4 * GDN_W, 4 * GDN_W + GDN_HEADS
    o_sz = o_a + GDN_HEADS
    o_xbc = o_sz + SSM_W
    o_dt = o_xbc + SSM_W + SSM_BC_W
    pad = jnp.zeros((d, SMALL_W - 2 * GDN_HEADS - SSM_HEADS), w_in.dtype)
    w_in_r = jnp.concatenate([
        w_in[:, o_qkv:o_z], w_in[:, o_xbc:o_dt], w_in[:, o_z:o_b], w_in[:, o_sz:o_xbc],
        w_in[:, o_b:o_a], w_in[:, o_a:o_sz], w_in[:, o_dt:o_dt + SSM_HEADS], pad], axis=1).astype(BF16)

    head_params = jnp.zeros((8, LANES), F32)
    head_params = head_params.at[0, GDN_A_LANE:GDN_A_LANE + GDN_HEADS].set(gdn_dt_bias)
    head_params = head_params.at[0, SSM_DT_LANE:SSM_DT_LANE + SSM_HEADS].set(ssm_dt_bias)
    head_params = head_params.at[1, GDN_A_LANE:GDN_A_LANE + GDN_HEADS].set(gdn_a_log)
    head_params = head_params.at[1, SSM_DT_LANE:SSM_DT_LANE + SSM_HEADS].set(ssm_a_log)

    d_exp = jnp.repeat(ssm_d, SSM_HEAD_DIM).reshape(1, SSM_W)
    sel = np.zeros((LANES, SSM_W), np.float32)
    for hd in range(SSM_HEADS):
        sel[SSM_DT_LANE + hd, hd * SSM_HEAD_DIM:(hd + 1) * SSM_HEAD_DIM] = 1.0
    sel2 = jnp.asarray(np.concatenate([sel, sel], axis=0), BF16)
    eye, negm, tril, levels = _block_constants(tb)
    n_levels = levels.shape[0]

    scratch = [
        pltpu.VMEM((PRE_BUFFERS, TAIL_ROWS + tb, CONV_CHUNK), F32),
        pltpu.VMEM((TAIL_ROWS, CONV_W), F32),
        pltpu.VMEM((tb, 3 * GDN_W), F32),
        pltpu.VMEM((tb, SSM_W + SSM_BC_W), F32),
        pltpu.VMEM((tb, GATE_W), F32),
        pltpu.VMEM((tb, SSM_W), F32),
        pltpu.VMEM((tb, SSM_W), F32),
        pltpu.VMEM((tb, SSM_W), F32),
        pltpu.VMEM((SSM_GROUPS, tb, tb), F32),
        pltpu.VMEM((tb, GATE_W), BF16),
        pltpu.VMEM((GDN_HEADS, tb, tb), BF16),
        pltpu.VMEM((GDN_HEADS, tb, tb), BF16),
        pltpu.VMEM((GDN_HEADS, tb, tb), BF16),
        pltpu.VMEM((GDN_HEADS, tb, 2 * GDN_HEAD_DIM), BF16),
        pltpu.VMEM((GDN_HEADS, 2 * tb, GDN_HEAD_DIM), BF16),
        pltpu.VMEM((GDN_HEADS, tb, GDN_HEAD_DIM), BF16),
        pltpu.VMEM((GDN_HEADS, tb, GDN_HEAD_DIM), F32),
        pltpu.VMEM((GDN_HEADS, tb, GDN_HEAD_DIM), F32),
        pltpu.VMEM((GDN_HEADS, tb, GDN_HEAD_DIM), BF16),
        pltpu.VMEM((GDN_HEADS, GDN_HEAD_DIM, GDN_HEAD_DIM), F32),
        pltpu.VMEM((SSM_GROUPS, SSM_STATE, SSM_GROUP_W), F32),
    ]
    return pl.pallas_call(
        functools.partial(_mixer_kernel, tb=tb),
        grid=(bsz, seqlen // tb),
        in_specs=[
            pl.BlockSpec((1, tb, d), lambda b, j: (b, j, 0)),
            _resident((1, d)),
            _resident((d, IN_W_PADDED)),
            _resident((CONV_K, 3 * GDN_W)),
            _resident((CONV_K, SSM_W + SSM_BC_W)),
            _resident((1, SSM_W + SSM_BC_W)),
            _resident((8, LANES)),
            _resident((1, GDN_HEAD_DIM)),
            _resident((1, SSM_W)),
            _resident((1, SSM_W)),
            _resident((2 * LANES, SSM_W)),
            _resident((tb, tb)),
            _resident((tb, tb)),
            _resident((tb, tb)),
            _resident((n_levels, tb, tb)),
            _resident((GATE_W, d)),
        ],
        out_specs=pl.BlockSpec((1, tb, d), lambda b, j: (b, j, 0)),
        out_shape=jax.ShapeDtypeStruct((bsz, seqlen, d), F32),
        scratch_shapes=scratch,
        compiler_params=pltpu.CompilerParams(
            dimension_semantics=("arbitrary", "arbitrary"), vmem_limit_bytes=VMEM_LIMIT_BYTES),
    )(x, norm_w.reshape(1, d), w_in_r, gdn_conv_w, ssm_conv_w, ssm_conv_b.reshape(1, -1), head_params,
      gdn_norm_w.reshape(1, -1), d_exp, ssm_norm_w.reshape(1, -1), sel2,
      jnp.asarray(eye), jnp.asarray(negm), jnp.asarray(tril, BF16), jnp.asarray(levels, BF16),
      w_out.astype(BF16))


def kernel(x, ffn1_norm, ffn1_w_gate, ffn1_w_up, ffn1_w_down, mix_norm, w_in, gdn_conv_w, gdn_a_log,
           gdn_dt_bias, gdn_norm_w, ssm_conv_w, ssm_conv_b, ssm_a_log, ssm_dt_bias, ssm_d, ssm_norm_w,
           w_out, ffn2_norm, ffn2_w_gate, ffn2_w_up, ffn2_w_down, final_norm):
    bsz, seqlen, d = x.shape
    depth = ffn1_norm.shape[0]
    for i in range(depth):
        x2 = _ffn(x.reshape(bsz * seqlen, d), ffn1_norm[i], ffn1_w_gate[i], ffn1_w_up[i],
                  ffn1_w_down[i], final_norm, False)
        x = _mixer(x2.reshape(bsz, seqlen, d), mix_norm[i], w_in[i], gdn_conv_w[i], gdn_a_log[i],
                   gdn_dt_bias[i], gdn_norm_w[i], ssm_conv_w[i], ssm_conv_b[i], ssm_a_log[i],
                   ssm_dt_bias[i], ssm_d[i], ssm_norm_w[i], w_out[i])
        x2 = _ffn(x.reshape(bsz * seqlen, d), ffn2_norm[i], ffn2_w_gate[i], ffn2_w_up[i],
                  ffn2_w_down[i], final_norm, i == depth - 1)
        x = x2.reshape(bsz, seqlen, d)
    return x
```

```python
import functools

import jax
import jax.numpy as jnp
import numpy as np
from jax import lax
from jax.experimental import pallas as pl
from jax.experimental.pallas import tpu as pltpu

F32 = jnp.float32
BF16 = jnp.bfloat16

D_MODEL = 1024
D_FF = 2816
CONV_K = 4
EPS = 1e-6
GDN_HEADS = 8
GDN_HEAD_DIM = 128
GDN_W = GDN_HEADS * GDN_HEAD_DIM
SSM_HEADS = 16
SSM_HEAD_DIM = 64
SSM_GROUPS = 2
SSM_STATE = 128
SSM_W = SSM_HEADS * SSM_HEAD_DIM
SSM_GROUP_W = SSM_W // SSM_GROUPS
SSM_BC_W = 2 * SSM_GROUPS * SSM_STATE

LANES = 128
CONV_W = 3 * GDN_W + SSM_W + SSM_BC_W
GATE_W = GDN_W + SSM_W
SMALL_W = LANES
IN_W_PADDED = CONV_W + GATE_W + SMALL_W
XBC_OFF = 3 * GDN_W
BETA_LANE = 0
GDN_A_LANE = GDN_HEADS
SSM_DT_LANE = 2 * GDN_HEADS
TAIL_ROWS = 8

FFN_TILE = 512
CONV_CHUNK = 512
PRE_BUFFERS = 2
MIX_BLOCK = 256
BLOCKWISE_MIN = 32
VMEM_LIMIT_BYTES = 56 * 1024 * 1024

LOG2E = 1.4426950408889634


def _rms(x, w):
    ms = jnp.mean(x * x, axis=-1, keepdims=True)
    return x * lax.rsqrt(ms + EPS) * w


def _silu(x):
    return x * jax.nn.sigmoid(x)


def _softplus(x):
    return jnp.maximum(x, 0.0) + jnp.log1p(jnp.exp(-jnp.abs(x)))


def _dot(a, b):
    return jnp.dot(a, b, preferred_element_type=F32)


def _dot_nt(a, b):
    return lax.dot_general(a, b, (((1,), (1,)), ((), ())), preferred_element_type=F32)


def _dot_tn(a, b):
    return lax.dot_general(a, b, (((0,), (0,)), ((), ())), preferred_element_type=F32)


def _ffn_kernel(x_ref, nw_ref, wg_ref, wu_ref, wd_ref, fw_ref, o_ref, *, final_norm):
    x = x_ref[...]
    h = _rms(x, nw_ref[...]).astype(BF16)
    a = _dot(h, wg_ref[...])
    b = _dot(h, wu_ref[...])
    g = (_silu(a) * b).astype(BF16)
    out = x + 0.5 * _dot(g, wd_ref[...])
    if final_norm:
        out = _rms(out, fw_ref[...])
    o_ref[...] = out


def _resident(shape):
    return pl.BlockSpec(shape, lambda *_: (0,) * len(shape), pipeline_mode=pl.Buffered(1))


def _ffn(x2d, norm_w, w_gate, w_up, w_down, final_w, final_norm):
    t, d = x2d.shape
    tm = min(FFN_TILE, t)
    assert t % tm == 0
    return pl.pallas_call(
        functools.partial(_ffn_kernel, final_norm=final_norm),
        grid=(t // tm,),
        in_specs=[
            pl.BlockSpec((tm, d), lambda i: (i, 0)),
            _resident((1, d)),
            _resident((d, D_FF)),
            _resident((d, D_FF)),
            _resident((D_FF, d)),
            _resident((1, d)),
        ],
        out_specs=pl.BlockSpec((tm, d), lambda i: (i, 0)),
        out_shape=jax.ShapeDtypeStruct((t, d), F32),
        compiler_params=pltpu.CompilerParams(
            dimension_semantics=("arbitrary",), vmem_limit_bytes=VMEM_LIMIT_BYTES),
    )(x2d, norm_w.reshape(1, d), w_gate.astype(BF16), w_up.astype(BF16), w_down.astype(BF16),
      final_w.reshape(1, d))


def _block_constants(tb):
    r = np.arange(tb)[:, None]
    c = np.arange(tb)[None, :]
    eye = (r == c).astype(np.float32)
    tril = (c <= r).astype(np.float32)
    negm = np.where(c <= r, 0.0, -np.inf).astype(np.float32)
    levels = [((r // 2 == c // 2) & (r > c)).astype(np.float32)]
    b = 2
    while b < BLOCKWISE_MIN:
        levels.append(-((r // (2 * b) == c // (2 * b)) & (r // b != c // b) & (r > c)).astype(np.float32))
        b *= 2
    return eye, negm, tril, np.stack(levels)


def _mixer_kernel(x_ref, nw_ref, win_ref, gcw_ref, scw_ref, scb_ref, hp_ref, gnw_ref, dexp_ref,
                  snw_ref, sel_ref, eye_ref, negm_ref, tril_ref, lvl_ref, wout_ref, o_ref,
                  pre, tail, qkv, xbc, gates, xdt_s, xe_s, dine_s, cb_s, mix_s,
                  a_s, t_s, qk_s, rhs_s, lhs_s, kd_s, u_s, oq_s, vn_s, sg, ss, *, tb):
    n_levels = lvl_ref.shape[0]
    n_conv_chunks = CONV_W // CONV_CHUNK

    @pl.when(pl.program_id(1) == 0)
    def _():
        sg[...] = jnp.zeros_like(sg)
        ss[...] = jnp.zeros_like(ss)
        tail[...] = jnp.zeros_like(tail)

    x = x_ref[0]
    h = _rms(x, nw_ref[...]).astype(BF16)

    small = _dot(h, win_ref[:, CONV_W + GATE_W:IN_W_PADDED])
    sp = _softplus(small + hp_ref[0:1, :])
    sig = jax.nn.sigmoid(small)
    logdec = -jnp.exp(hp_ref[1:2, :]) * sp

    ld_hi = logdec.astype(BF16)
    ld_r = logdec - ld_hi.astype(F32)
    ld_mid = ld_r.astype(BF16)
    ld_lo = (ld_r - ld_mid.astype(F32)).astype(BF16)
    cs3 = _dot(tril_ref[...], jnp.concatenate([ld_hi, ld_mid, ld_lo], axis=1))
    cs = (cs3[:, 0:LANES] + cs3[:, LANES:2 * LANES]) + cs3[:, 2 * LANES:3 * LANES]
    din = jnp.exp(cs)
    toe = jnp.exp(cs[tb - 1:tb, :] - cs)
    din_last = din[tb - 1:tb, :]
    cs_l2 = cs * LOG2E
    cs_l2t = cs_l2.T

    def expand(v):
        hi = v.astype(BF16)
        lo_ = (v - hi.astype(F32)).astype(BF16)
        return _dot(jnp.concatenate([hi, lo_], axis=1), sel_ref[...])

    for j in range(n_conv_chunks):
        lo = j * CONV_CHUNK
        buf = pre.at[j % PRE_BUFFERS]
        buf[0:TAIL_ROWS, :] = tail[:, lo:lo + CONV_CHUNK]
        buf[TAIL_ROWS:TAIL_ROWS + tb, :] = _dot(h, win_ref[:, lo:lo + CONV_CHUNK])
        if lo < XBC_OFF:
            w_ref, wlo, bias = gcw_ref, lo, None
        else:
            w_ref, wlo = scw_ref, lo - XBC_OFF
            bias = scb_ref[:, wlo:wlo + CONV_CHUNK]
        w0, w1, w2, w3 = (w_ref[k:k + 1, wlo:wlo + CONV_CHUNK] for k in range(CONV_K))
        p_all = buf[...]
        p_m2 = pltpu.roll(p_all, 2, 0)
        odd = pltpu.roll(w2 * p_all + w0 * p_m2, 1, 0)
        acc = (w3 * p_all[TAIL_ROWS:] + w1 * p_m2[TAIL_ROWS:]) + odd[TAIL_ROWS:]
        if bias is not None:
            acc = acc + bias
        y = _silu(acc)
        if lo < XBC_OFF:
            qkv[:, lo:lo + CONV_CHUNK] = y
        else:
            xbc[:, wlo:wlo + CONV_CHUNK] = y
        tail[:, lo:lo + CONV_CHUNK] = buf[tb:tb + TAIL_ROWS, :]

    half = lax.broadcasted_iota(jnp.int32, (tb, LANES), 1) // SSM_HEAD_DIM
    pairs_per_group = SSM_GROUP_W // LANES

    def ssd_pair(pi):
        g, pp = divmod(pi, pairs_per_group)
        csl = slice(g * SSM_GROUP_W + pp * LANES, g * SSM_GROUP_W + (pp + 1) * LANES)
        xpair = xdt_s[:, csl]
        acc = dine_s[:, csl]
        for tt in range(LANES // SSM_HEAD_DIM):
            hl = SSM_DT_LANE + g * (SSM_HEADS // SSM_GROUPS) + pp * (LANES // SSM_HEAD_DIM) + tt
            seg = jnp.exp2(cs_l2[:, hl:hl + 1] - cs_l2t[hl:hl + 1, :] + negm_ref[...])
            acc = acc + _dot((cb_s[g] * seg).astype(BF16), jnp.where(half == tt, xpair, 0.0).astype(BF16))
        y = acc + xbc[:, csl] * dexp_ref[:, csl]
        gsl2 = slice(GDN_W + g * SSM_GROUP_W + pp * LANES, GDN_W + g * SSM_GROUP_W + (pp + 1) * LANES)
        xe_s[:, csl] = y * _silu(gates[:, gsl2])

    eye_b = eye_ref[...].astype(BF16)
    gate_piece = GATE_W // GDN_HEADS
    for hd in range(GDN_HEADS):
        glo = hd * gate_piece
        gates[:, glo:glo + gate_piece] = _dot(h, win_ref[:, CONV_W + glo:CONV_W + glo + gate_piece])
        sl = slice(hd * GDN_HEAD_DIM, (hd + 1) * GDN_HEAD_DIM)
        q = qkv[:, sl]
        k = qkv[:, GDN_W + hd * GDN_HEAD_DIM:GDN_W + (hd + 1) * GDN_HEAD_DIM]
        v = qkv[:, 2 * GDN_W + hd * GDN_HEAD_DIM:2 * GDN_W + (hd + 1) * GDN_HEAD_DIM]
        qn = q * (lax.rsqrt(jnp.sum(q * q, axis=-1, keepdims=True) + EPS) * GDN_HEAD_DIM ** -0.5)
        kn = k * lax.rsqrt(jnp.sum(k * k, axis=-1, keepdims=True) + EPS)
        gl = GDN_A_LANE + hd
        beta = sig[:, BETA_LANE + hd:BETA_LANE + hd + 1]
        eg = din[:, gl:gl + 1]
        decay = jnp.exp2(cs_l2[:, gl:gl + 1] - cs_l2t[gl:gl + 1, :] + negm_ref[...])
        knb = kn.astype(BF16)
        kq = _dot_nt(jnp.concatenate([knb, qn.astype(BF16)], axis=0), knb)
        a_b = (beta * kq[0:tb] * decay).astype(BF16)
        a_s[hd] = a_b
        t_s[hd] = eye_b - a_b * lvl_ref[0]
        qk_s[hd] = (kq[tb:2 * tb] * decay).astype(BF16)
        rhs_s[hd] = jnp.concatenate([v * beta, kn * (beta * eg)], axis=1).astype(BF16)
        lhs_s[hd, tb:2 * tb, :] = (qn * eg).astype(BF16)
        kd_s[hd] = (kn * toe[:, gl:gl + 1]).astype(BF16)

    xdt = xbc[:, 0:SSM_W] * expand(sp)
    xdt_s[...] = xdt
    xe_s[...] = xdt * expand(toe)
    dine_s[...] = expand(din)
    for g in range(SSM_GROUPS):
        gsl = slice(g * SSM_GROUP_W, (g + 1) * SSM_GROUP_W)
        bg = xbc[:, SSM_W + g * SSM_STATE:SSM_W + (g + 1) * SSM_STATE].astype(BF16)
        cg = xbc[:, SSM_W + (SSM_GROUPS + g) * SSM_STATE:
                 SSM_W + (SSM_GROUPS + g + 1) * SSM_STATE].astype(BF16)
        cb_s[g] = _dot_nt(cg, bg)
        state = ss[g]
        ss[g] = state * dine_s[tb - 1:tb, gsl] + _dot_tn(bg, xe_s[:, gsl].astype(BF16))
        dine_s[:, gsl] = _dot(cg, state.astype(BF16)) * dine_s[:, gsl]

    n_pairs = SSM_GROUPS * pairs_per_group
    top_stages = 2 * ((tb // BLOCKWISE_MIN).bit_length() - 1)
    pairs_left = list(range(n_pairs))

    def some_pairs(n):
        for _ in range(min(n, len(pairs_left))):
            ssd_pair(pairs_left.pop(0))

    for lv in range(1, n_levels):
        for hd in range(GDN_HEADS):
            t_b = t_s[hd]
            xneg = _dot(a_s[hd] * lvl_ref[lv], t_b)
            t_s[hd] = t_b + _dot(t_b, xneg.astype(BF16)).astype(BF16)
        if n_levels - lv <= n_pairs - top_stages:
            some_pairs(1)

    def on_k_axis(blk, at):
        parts = []
        if at:
            parts.append(jnp.zeros((at, LANES), BF16))
        parts.append(blk)
        if at + blk.shape[0] < LANES:
            parts.append(jnp.zeros((LANES - at - blk.shape[0], LANES), BF16))
        return jnp.concatenate(parts, axis=0)

    b = BLOCKWISE_MIN
    while b < LANES:
        blocks = [(hd, r0, r0 % LANES) for hd in range(GDN_HEADS) for r0 in range(0, tb, 2 * b)]
        x_blks = []
        for hd, r0, c0 in blocks:
            tile = slice(r0 - c0, r0 - c0 + LANES)
            x_blks.append(_dot(a_s[hd, r0 + b:r0 + 2 * b, tile], on_k_axis(t_s[hd, r0:r0 + b, tile], c0)))
        some_pairs(1)
        for (hd, r0, c0), x_blk in zip(blocks, x_blks):
            tile = slice(r0 - c0, r0 - c0 + LANES)
            t_bot = t_s[hd, r0 + b:r0 + 2 * b, tile]
            p_blk = _dot(t_bot, on_k_axis(x_blk.astype(BF16), c0 + b))
            t_s[hd, r0 + b:r0 + 2 * b, tile] = t_bot - p_blk.astype(BF16)
        some_pairs(1)
        b *= 2
    blocks = [(hd, i) for hd in range(GDN_HEADS) for i in range(tb // (2 * LANES))]

    def rows128(i):
        return slice(2 * i * LANES, (2 * i + 1) * LANES), slice((2 * i + 1) * LANES, (2 * i + 2) * LANES)

    x_blks = []
    for hd, i in blocks:
        top, bot = rows128(i)
        x_blks.append(_dot(a_s[hd, bot, top], t_s[hd, top, top]))
    some_pairs(1)
    for (hd, i), x_blk in zip(blocks, x_blks):
        top, bot = rows128(i)
        t_s[hd, bot, top] = (-_dot(t_s[hd, bot, bot], x_blk.astype(BF16))).astype(BF16)
    some_pairs(n_pairs)

    for hd in range(GDN_HEADS):
        uw = _dot(t_s[hd], rhs_s[hd])
        u_s[hd] = uw[:, 0:GDN_HEAD_DIM]
        lhs_s[hd, 0:tb, :] = uw[:, GDN_HEAD_DIM:2 * GDN_HEAD_DIM].astype(BF16)

    for hd in range(GDN_HEADS):
        wq = _dot(lhs_s[hd], sg[hd].astype(BF16))
        vn_s[hd] = (u_s[hd] - wq[0:tb]).astype(BF16)
        oq_s[hd] = wq[tb:2 * tb]

    for hd in range(GDN_HEADS):
        sl = slice(hd * GDN_HEAD_DIM, (hd + 1) * GDN_HEAD_DIM)
        gl = GDN_A_LANE + hd
        v_new = vn_s[hd]
        o = oq_s[hd] + _dot(qk_s[hd], v_new)
        sg[hd] = sg[hd] * din_last[:, gl:gl + 1] + _dot_tn(kd_s[hd], v_new)
        mix_s[:, sl] = (_rms(o, gnw_ref[...]) * _silu(gates[:, sl])).astype(BF16)

    for g in range(SSM_GROUPS):
        gsl = slice(g * SSM_GROUP_W, (g + 1) * SSM_GROUP_W)
        mix_s[:, GDN_W + g * SSM_GROUP_W:GDN_W + (g + 1) * SSM_GROUP_W] = (
            _rms(xe_s[:, gsl], snw_ref[:, gsl]).astype(BF16))

    o_ref[0] = x + _dot(mix_s[...], wout_ref[...])


def _mixer(x, norm_w, w_in, gdn_conv_w, gdn_a_log, gdn_dt_bias, gdn_norm_w,
           ssm_conv_w, ssm_conv_b, ssm_a_log, ssm_dt_bias, ssm_d, ssm_norm_w, w_out):
    bsz, seqlen, d = x.shape
    tb = min(MIX_BLOCK, seqlen)
    assert seqlen % tb == 0 and tb & (tb - 1) == 0 and tb % (2 * LANES) == 0

    o_qkv, o_z, o_b, o_a = 0, 3 * GDN_W, 4 * GDN_W, 4 * GDN_W + GDN_HEADS
    o_sz = o_a + GDN_HEADS
    o_xbc = o_sz + SSM_W
    o_dt = o_xbc + SSM_W + SSM_BC_W
    pad = jnp.zeros((d, SMALL_W - 2 * GDN_HEADS - SSM_HEADS), w_in.dtype)
    w_in_r = jnp.concatenate([
        w_in[:, o_qkv:o_z], w_in[:, o_xbc:o_dt], w_in[:, o_z:o_b], w_in[:, o_sz:o_xbc],
        w_in[:, o_b:o_a], w_in[:, o_a:o_sz], w_in[:, o_dt:o_dt + SSM_HEADS], pad], axis=1).astype(BF16)

    head_params = jnp.zeros((8, LANES), F32)
    head_params = head_params.at[0, GDN_A_LANE:GDN_A_LANE + GDN_HEADS].set(gdn_dt_bias)
    head_params = head_params.at[0, SSM_DT_LANE:SSM_DT_LANE + SSM_HEADS].set(ssm_dt_bias)
    head_params = head_params.at[1, GDN_A_LANE:GDN_A_LANE + GDN_HEADS].set(gdn_a_log)
    head_params = head_params.at[1, SSM_DT_LANE:SSM_DT_LANE + SSM_HEADS].set(ssm_a_log)

    d_exp = jnp.repeat(ssm_d, SSM_HEAD_DIM).reshape(1, SSM_W)
    sel = np.zeros((LANES, SSM_W), np.float32)
    for hd in range(SSM_HEADS):
        sel[SSM_DT_LANE + hd, hd * SSM_HEAD_DIM:(hd + 1) * SSM_HEAD_DIM] = 1.0
    sel2 = jnp.asarray(np.concatenate([sel, sel], axis=0), BF16)
    eye, negm, tril, levels = _block_constants(tb)
    n_levels = levels.shape[0]

    scratch = [
        pltpu.VMEM((PRE_BUFFERS, TAIL_ROWS + tb, CONV_CHUNK), F32),
        pltpu.VMEM((TAIL_ROWS, CONV_W), F32),
        pltpu.VMEM((tb, 3 * GDN_W), F32),
        pltpu.VMEM((tb, SSM_W + SSM_BC_W), F32),
        pltpu.VMEM((tb, GATE_W), F32),
        pltpu.VMEM((tb, SSM_W), F32),
        pltpu.VMEM((tb, SSM_W), F32),
        pltpu.VMEM((tb, SSM_W), F32),
        pltpu.VMEM((SSM_GROUPS, tb, tb), F32),
        pltpu.VMEM((tb, GATE_W), BF16),
        pltpu.VMEM((GDN_HEADS, tb, tb), BF16),
        pltpu.VMEM((GDN_HEADS, tb, tb), BF16),
        pltpu.VMEM((GDN_HEADS, tb, tb), BF16),
        pltpu.VMEM((GDN_HEADS, tb, 2 * GDN_HEAD_DIM), BF16),
        pltpu.VMEM((GDN_HEADS, 2 * tb, GDN_HEAD_DIM), BF16),
        pltpu.VMEM((GDN_HEADS, tb, GDN_HEAD_DIM), BF16),
        pltpu.VMEM((GDN_HEADS, tb, GDN_HEAD_DIM), F32),
        pltpu.VMEM((GDN_HEADS, tb, GDN_HEAD_DIM), F32),
        pltpu.VMEM((GDN_HEADS, tb, GDN_HEAD_DIM), BF16),
        pltpu.VMEM((GDN_HEADS, GDN_HEAD_DIM, GDN_HEAD_DIM), F32),
        pltpu.VMEM((SSM_GROUPS, SSM_STATE, SSM_GROUP_W), F32),
    ]
    return pl.pallas_call(
        functools.partial(_mixer_kernel, tb=tb),
        grid=(bsz, seqlen // tb),
        in_specs=[
            pl.BlockSpec((1, tb, d), lambda b, j: (b, j, 0)),
            _resident((1, d)),
            _resident((d, IN_W_PADDED)),
            _resident((CONV_K, 3 * GDN_W)),
            _resident((CONV_K, SSM_W + SSM_BC_W)),
            _resident((1, SSM_W + SSM_BC_W)),
            _resident((8, LANES)),
            _resident((1, GDN_HEAD_DIM)),
            _resident((1, SSM_W)),
            _resident((1, SSM_W)),
            _resident((2 * LANES, SSM_W)),
            _resident((tb, tb)),
            _resident((tb, tb)),
            _resident((tb, tb)),
            _resident((n_levels, tb, tb)),
            _resident((GATE_W, d)),
        ],
        out_specs=pl.BlockSpec((1, tb, d), lambda b, j: (b, j, 0)),
        out_shape=jax.ShapeDtypeStruct((bsz, seqlen, d), F32),
        scratch_shapes=scratch,
        compiler_params=pltpu.CompilerParams(
            dimension_semantics=("arbitrary", "arbitrary"), vmem_limit_bytes=VMEM_LIMIT_BYTES),
    )(x, norm_w.reshape(1, d), w_in_r, gdn_conv_w, ssm_conv_w, ssm_conv_b.reshape(1, -1), head_params,
      gdn_norm_w.reshape(1, -1), d_exp, ssm_norm_w.reshape(1, -1), sel2,
      jnp.asarray(eye), jnp.asarray(negm), jnp.asarray(tril, BF16), jnp.asarray(levels, BF16),
      w_out.astype(BF16))


def kernel(x, ffn1_norm, ffn1_w_gate, ffn1_w_up, ffn1_w_down, mix_norm, w_in, gdn_conv_w, gdn_a_log,
           gdn_dt_bias, gdn_norm_w, ssm_conv_w, ssm_conv_b, ssm_a_log, ssm_dt_bias, ssm_d, ssm_norm_w,
           w_out, ffn2_norm, ffn2_w_gate, ffn2_w_up, ffn2_w_down, final_norm):
    bsz, seqlen, d = x.shape
    depth = ffn1_norm.shape[0]
    for i in range(depth):
        x2 = _ffn(x.reshape(bsz * seqlen, d), ffn1_norm[i], ffn1_w_gate[i], ffn1_w_up[i],
                  ffn1_w_down[i], final_norm, False)
        x = _mixer(x2.reshape(bsz, seqlen, d), mix_norm[i], w_in[i], gdn_conv_w[i], gdn_a_log[i],
                   gdn_dt_bias[i], gdn_norm_w[i], ssm_conv_w[i], ssm_conv_b[i], ssm_a_log[i],
                   ssm_dt_bias[i], ssm_d[i], ssm_norm_w[i], w_out[i])
        x2 = _ffn(x.reshape(bsz * seqlen, d), ffn2_norm[i], ffn2_w_gate[i], ffn2_w_up[i],
                  ffn2_w_down[i], final_norm, i == depth - 1)
        x = x2.reshape(bsz, seqlen, d)
    return x
```

```python
import functools

import jax
import jax.numpy as jnp
import numpy as np
from jax import lax
from jax.experimental import pallas as pl
from jax.experimental.pallas import tpu as pltpu

F32 = jnp.float32
BF16 = jnp.bfloat16

D_MODEL = 1024
D_FF = 2816
CONV_K = 4
EPS = 1e-6
GDN_HEADS = 8
GDN_HEAD_DIM = 128
GDN_W = GDN_HEADS * GDN_HEAD_DIM
SSM_HEADS = 16
SSM_HEAD_DIM = 64
SSM_GROUPS = 2
SSM_STATE = 128
SSM_W = SSM_HEADS * SSM_HEAD_DIM
SSM_GROUP_W = SSM_W // SSM_GROUPS
SSM_BC_W = 2 * SSM_GROUPS * SSM_STATE

LANES = 128
CONV_W = 3 * GDN_W + SSM_W + SSM_BC_W
GATE_W = GDN_W + SSM_W
SMALL_W = LANES
IN_W_PADDED = CONV_W + GATE_W + SMALL_W
XBC_OFF = 3 * GDN_W
BETA_LANE = 0
GDN_A_LANE = GDN_HEADS
SSM_DT_LANE = 2 * GDN_HEADS
TAIL_ROWS = 8

FFN_TILE = 1024
FFN_SUBTILE = 512
CONV_CHUNK = 512
PRE_BUFFERS = 2
MIX_BLOCK = 256
BLOCKWISE_MIN = 32
VMEM_LIMIT_BYTES = 56 * 1024 * 1024

LOG2E = 1.4426950408889634


def _rms(x, w):
    ms = jnp.mean(x * x, axis=-1, keepdims=True)
    return x * lax.rsqrt(ms + EPS) * w


def _silu(x):
    return x * jax.nn.sigmoid(x)


def _softplus(x):
    return jnp.maximum(x, 0.0) + jnp.log1p(jnp.exp(-jnp.abs(x)))


def _dot(a, b):
    return jnp.dot(a, b, preferred_element_type=F32)


def _dot_nt(a, b):
    return lax.dot_general(a, b, (((1,), (1,)), ((), ())), preferred_element_type=F32)


def _dot_tn(a, b):
    return lax.dot_general(a, b, (((0,), (0,)), ((), ())), preferred_element_type=F32)


def _ffn_kernel(x_ref, nw_ref, wg_ref, wu_ref, wd_ref, fw_ref, o_ref, *, final_norm):
    for r in range(0, x_ref.shape[0], FFN_SUBTILE):
        rows = slice(r, r + FFN_SUBTILE)
        x = x_ref[rows, :]
        h = _rms(x, nw_ref[...]).astype(BF16)
        a = _dot(h, wg_ref[...])
        b = _dot(h, wu_ref[...])
        g = (_silu(a) * b).astype(BF16)
        out = x + 0.5 * _dot(g, wd_ref[...])
        if final_norm:
            out = _rms(out, fw_ref[...])
        o_ref[rows, :] = out


def _resident(shape):
    return pl.BlockSpec(shape, lambda *_: (0,) * len(shape), pipeline_mode=pl.Buffered(1))


def _ffn(x2d, norm_w, w_gate, w_up, w_down, final_w, final_norm):
    t, d = x2d.shape
    tm = min(FFN_TILE, t)
    assert t % tm == 0 and tm % FFN_SUBTILE == 0
    return pl.pallas_call(
        functools.partial(_ffn_kernel, final_norm=final_norm),
        grid=(t // tm,),
        in_specs=[
            pl.BlockSpec((tm, d), lambda i: (i, 0)),
            _resident((1, d)),
            _resident((d, D_FF)),
            _resident((d, D_FF)),
            _resident((D_FF, d)),
            _resident((1, d)),
        ],
        out_specs=pl.BlockSpec((tm, d), lambda i: (i, 0)),
        out_shape=jax.ShapeDtypeStruct((t, d), F32),
        compiler_params=pltpu.CompilerParams(
            dimension_semantics=("arbitrary",), vmem_limit_bytes=VMEM_LIMIT_BYTES),
    )(x2d, norm_w.reshape(1, d), w_gate.astype(BF16), w_up.astype(BF16), w_down.astype(BF16),
      final_w.reshape(1, d))


def _block_constants(tb):
    r = np.arange(tb)[:, None]
    c = np.arange(tb)[None, :]
    eye = (r == c).astype(np.float32)
    tril = (c <= r).astype(np.float32)
    negm = np.where(c <= r, 0.0, -np.inf).astype(np.float32)
    levels = [((r // 2 == c // 2) & (r > c)).astype(np.float32)]
    b = 2
    while b < BLOCKWISE_MIN:
        levels.append(-((r // (2 * b) == c // (2 * b)) & (r // b != c // b) & (r > c)).astype(np.float32))
        b *= 2
    return eye, negm, tril, np.stack(levels)


def _mixer_kernel(x_ref, nw_ref, win_ref, gcw_ref, scw_ref, scb_ref, hp_ref, gnw_ref, dexp_ref,
                  snw_ref, sel_ref, eye_ref, negm_ref, tril_ref, lvl_ref, wout_ref, o_ref,
                  pre, tail, qkv, xbc, gates, xdt_s, xe_s, dine_s, cb_s, mix_s,
                  a_s, t_s, qk_s, rhs_s, lhs_s, kd_s, u_s, oq_s, vn_s, sg, ss, *, tb):
    n_levels = lvl_ref.shape[0]
    n_conv_chunks = CONV_W // CONV_CHUNK

    @pl.when(pl.program_id(1) == 0)
    def _():
        sg[...] = jnp.zeros_like(sg)
        ss[...] = jnp.zeros_like(ss)
        tail[...] = jnp.zeros_like(tail)

    x = x_ref[0]
    h = _rms(x, nw_ref[...]).astype(BF16)

    def expand(v):
        hi = v.astype(BF16)
        lo_ = (v - hi.astype(F32)).astype(BF16)
        return _dot(jnp.concatenate([hi, lo_], axis=1), sel_ref[...])

    for j in range(n_conv_chunks):
        lo = j * CONV_CHUNK
        buf = pre.at[j % PRE_BUFFERS]
        buf[0:TAIL_ROWS, :] = tail[:, lo:lo + CONV_CHUNK]
        buf[TAIL_ROWS:TAIL_ROWS + tb, :] = _dot(h, win_ref[:, lo:lo + CONV_CHUNK])
        if lo < XBC_OFF:
            w_ref, wlo, bias = gcw_ref, lo, None
        else:
            w_ref, wlo = scw_ref, lo - XBC_OFF
            bias = scb_ref[:, wlo:wlo + CONV_CHUNK]
        w0, w1, w2, w3 = (w_ref[k:k + 1, wlo:wlo + CONV_CHUNK] for k in range(CONV_K))
        p_all = buf[...]
        p_m2 = pltpu.roll(p_all, 2, 0)
        odd = pltpu.roll(w2 * p_all + w0 * p_m2, 1, 0)
        acc = (w3 * p_all[TAIL_ROWS:] + w1 * p_m2[TAIL_ROWS:]) + odd[TAIL_ROWS:]
        if bias is not None:
            acc = acc + bias
        y = _silu(acc)
        if lo < XBC_OFF:
            qkv[:, lo:lo + CONV_CHUNK] = y
        else:
            xbc[:, wlo:wlo + CONV_CHUNK] = y
        tail[:, lo:lo + CONV_CHUNK] = buf[tb:tb + TAIL_ROWS, :]
        if j == 1:
            small = _dot(h, win_ref[:, CONV_W + GATE_W:IN_W_PADDED])
            sp = _softplus(small + hp_ref[0:1, :])
            sig = jax.nn.sigmoid(small)
            logdec = -jnp.exp(hp_ref[1:2, :]) * sp

            ld_hi = logdec.astype(BF16)
            ld_r = logdec - ld_hi.astype(F32)
            ld_mid = ld_r.astype(BF16)
            ld_lo = (ld_r - ld_mid.astype(F32)).astype(BF16)
            cs3 = _dot(tril_ref[...], jnp.concatenate([ld_hi, ld_mid, ld_lo], axis=1))
            cs = (cs3[:, 0:LANES] + cs3[:, LANES:2 * LANES]) + cs3[:, 2 * LANES:3 * LANES]
            din = jnp.exp(cs)
            toe = jnp.exp(cs[tb - 1:tb, :] - cs)
            din_last = din[tb - 1:tb, :]
            cs_l2 = cs * LOG2E
            cs_l2t = cs_l2.T

    half = lax.broadcasted_iota(jnp.int32, (tb, LANES), 1) // SSM_HEAD_DIM
    pairs_per_group = SSM_GROUP_W // LANES

    def ssd_pair(pi):
        g, pp = divmod(pi, pairs_per_group)
        csl = slice(g * SSM_GROUP_W + pp * LANES, g * SSM_GROUP_W + (pp + 1) * LANES)
        xpair = xdt_s[:, csl]
        acc = dine_s[:, csl]
        for tt in range(LANES // SSM_HEAD_DIM):
            hl = SSM_DT_LANE + g * (SSM_HEADS // SSM_GROUPS) + pp * (LANES // SSM_HEAD_DIM) + tt
            seg = jnp.exp2(cs_l2[:, hl:hl + 1] - cs_l2t[hl:hl + 1, :] + negm_ref[...])
            acc = acc + _dot((cb_s[g] * seg).astype(BF16), jnp.where(half == tt, xpair, 0.0).astype(BF16))
        y = acc + xbc[:, csl] * dexp_ref[:, csl]
        gsl2 = slice(GDN_W + g * SSM_GROUP_W + pp * LANES, GDN_W + g * SSM_GROUP_W + (pp + 1) * LANES)
        xe_s[:, csl] = y * _silu(gates[:, gsl2])

    eye_b = eye_ref[...].astype(BF16)
    gate_piece = GATE_W // GDN_HEADS
    for hd in range(GDN_HEADS):
        glo = hd * gate_piece
        gates[:, glo:glo + gate_piece] = _dot(h, win_ref[:, CONV_W + glo:CONV_W + glo + gate_piece])
        sl = slice(hd * GDN_HEAD_DIM, (hd + 1) * GDN_HEAD_DIM)
        q = qkv[:, sl]
        k = qkv[:, GDN_W + hd * GDN_HEAD_DIM:GDN_W + (hd + 1) * GDN_HEAD_DIM]
        v = qkv[:, 2 * GDN_W + hd * GDN_HEAD_DIM:2 * GDN_W + (hd + 1) * GDN_HEAD_DIM]
        qn = q * (lax.rsqrt(jnp.sum(q * q, axis=-1, keepdims=True) + EPS) * GDN_HEAD_DIM ** -0.5)
        kn = k * lax.rsqrt(jnp.sum(k * k, axis=-1, keepdims=True) + EPS)
        gl = GDN_A_LANE + hd
        beta = sig[:, BETA_LANE + hd:BETA_LANE + hd + 1]
        decay = jnp.exp2(cs_l2[:, gl:gl + 1] - cs_l2t[gl:gl + 1, :] + negm_ref[...])
        knb = kn.astype(BF16)
        kq = _dot_nt(jnp.concatenate([knb, qn.astype(BF16)], axis=0), knb)
        a_b = (beta * kq[0:tb] * decay).astype(BF16)
        a_s[hd] = a_b
        t_s[hd] = eye_b - a_b * lvl_ref[0]
        qk_s[hd] = (kq[tb:2 * tb] * decay).astype(BF16)
        qkv[:, sl] = qn
        qkv[:, GDN_W + hd * GDN_HEAD_DIM:GDN_W + (hd + 1) * GDN_HEAD_DIM] = kn

    def gdn_operands(hd):
        sl = slice(hd * GDN_HEAD_DIM, (hd + 1) * GDN_HEAD_DIM)
        qn = qkv[:, sl]
        kn = qkv[:, GDN_W + hd * GDN_HEAD_DIM:GDN_W + (hd + 1) * GDN_HEAD_DIM]
        v = qkv[:, 2 * GDN_W + hd * GDN_HEAD_DIM:2 * GDN_W + (hd + 1) * GDN_HEAD_DIM]
        gl = GDN_A_LANE + hd
        beta = sig[:, BETA_LANE + hd:BETA_LANE + hd + 1]
        eg = din[:, gl:gl + 1]
        rhs_s[hd] = jnp.concatenate([v * beta, kn * (beta * eg)], axis=1).astype(BF16)
        lhs_s[hd, tb:2 * tb, :] = (qn * eg).astype(BF16)
        kd_s[hd] = (kn * toe[:, gl:gl + 1]).astype(BF16)

    xdt = xbc[:, 0:SSM_W] * expand(sp)
    xdt_s[...] = xdt
    xe_s[...] = xdt * expand(toe)
    dine_s[...] = expand(din)
    for g in range(SSM_GROUPS):
        gsl = slice(g * SSM_GROUP_W, (g + 1) * SSM_GROUP_W)
        bg = xbc[:, SSM_W + g * SSM_STATE:SSM_W + (g + 1) * SSM_STATE].astype(BF16)
        cg = xbc[:, SSM_W + (SSM_GROUPS + g) * SSM_STATE:
                 SSM_W + (SSM_GROUPS + g + 1) * SSM_STATE].astype(BF16)
        cb_s[g] = _dot_nt(cg, bg)
        state = ss[g]
        ss[g] = state * dine_s[tb - 1:tb, gsl] + _dot_tn(bg, xe_s[:, gsl].astype(BF16))
        dine_s[:, gsl] = _dot(cg, state.astype(BF16)) * dine_s[:, gsl]

    n_pairs = SSM_GROUPS * pairs_per_group
    top_stages = 2 * ((tb // BLOCKWISE_MIN).bit_length() - 1)
    pairs_left = list(range(n_pairs))
    heads_per_level = -(-GDN_HEADS // (n_levels - 1))

    def some_pairs(n):
        for _ in range(min(n, len(pairs_left))):
            ssd_pair(pairs_left.pop(0))

    for lv in range(1, n_levels):
        for hd in range(GDN_HEADS):
            t_b = t_s[hd]
            xneg = _dot(a_s[hd] * lvl_ref[lv], t_b)
            t_s[hd] = t_b + _dot(t_b, xneg.astype(BF16)).astype(BF16)
        if n_levels - lv <= n_pairs - top_stages:
            some_pairs(1)
        for hd in range((lv - 1) * heads_per_level, min(lv * heads_per_level, GDN_HEADS)):
            gdn_operands(hd)

    def on_k_axis(blk, at):
        parts = []
        if at:
            parts.append(jnp.zeros((at, LANES), BF16))
        parts.append(blk)
        if at + blk.shape[0] < LANES:
            parts.append(jnp.zeros((LANES - at - blk.shape[0], LANES), BF16))
        return jnp.concatenate(parts, axis=0)

    b = BLOCKWISE_MIN
    while b < LANES:
        blocks = [(hd, r0, r0 % LANES) for hd in range(GDN_HEADS) for r0 in range(0, tb, 2 * b)]
        x_blks = []
        for hd, r0, c0 in blocks:
            tile = slice(r0 - c0, r0 - c0 + LANES)
            x_blks.append(_dot(a_s[hd, r0 + b:r0 + 2 * b, tile], on_k_axis(t_s[hd, r0:r0 + b, tile], c0)))
        some_pairs(1)
        for (hd, r0, c0), x_blk in zip(blocks, x_blks):
            tile = slice(r0 - c0, r0 - c0 + LANES)
            t_bot = t_s[hd, r0 + b:r0 + 2 * b, tile]
            p_blk = _dot(t_bot, on_k_axis(x_blk.astype(BF16), c0 + b))
            t_s[hd, r0 + b:r0 + 2 * b, tile] = t_bot - p_blk.astype(BF16)
        some_pairs(1)
        b *= 2
    blocks = [(hd, i) for hd in range(GDN_HEADS) for i in range(tb // (2 * LANES))]

    def rows128(i):
        return slice(2 * i * LANES, (2 * i + 1) * LANES), slice((2 * i + 1) * LANES, (2 * i + 2) * LANES)

    x_blks = []
    for hd, i in blocks:
        top, bot = rows128(i)
        x_blks.append(_dot(a_s[hd, bot, top], t_s[hd, top, top]))
    some_pairs(1)
    for (hd, i), x_blk in zip(blocks, x_blks):
        top, bot = rows128(i)
        t_s[hd, bot, top] = (-_dot(t_s[hd, bot, bot], x_blk.astype(BF16))).astype(BF16)
    some_pairs(n_pairs)

    for g in range(SSM_GROUPS):
        gsl = slice(g * SSM_GROUP_W, (g + 1) * SSM_GROUP_W)
        mix_s[:, GDN_W + g * SSM_GROUP_W:GDN_W + (g + 1) * SSM_GROUP_W] = (
            _rms(xe_s[:, gsl], snw_ref[:, gsl]).astype(BF16))
    o_ref[0] = x + _dot(mix_s[:, GDN_W:GATE_W], wout_ref[GDN_W:GATE_W, :])

    for hd in range(GDN_HEADS):
        uw = _dot(t_s[hd], rhs_s[hd])
        u_s[hd] = uw[:, 0:GDN_HEAD_DIM]
        lhs_s[hd, 0:tb, :] = uw[:, GDN_HEAD_DIM:2 * GDN_HEAD_DIM].astype(BF16)

    for hd in range(GDN_HEADS):
        wq = _dot(lhs_s[hd], sg[hd].astype(BF16))
        vn_s[hd] = (u_s[hd] - wq[0:tb]).astype(BF16)
        oq_s[hd] = wq[tb:2 * tb]

    for hd in range(GDN_HEADS):
        sl = slice(hd * GDN_HEAD_DIM, (hd + 1) * GDN_HEAD_DIM)
        gl = GDN_A_LANE + hd
        v_new = vn_s[hd]
        o = oq_s[hd] + _dot(qk_s[hd], v_new)
        sg[hd] = sg[hd] * din_last[:, gl:gl + 1] + _dot_tn(kd_s[hd], v_new)
        mix_s[:, sl] = (_rms(o, gnw_ref[...]) * _silu(gates[:, sl])).astype(BF16)

    o_ref[0] += _dot(mix_s[:, 0:GDN_W], wout_ref[0:GDN_W, :])


def _mixer(x, norm_w, w_in, gdn_conv_w, gdn_a_log, gdn_dt_bias, gdn_norm_w,
           ssm_conv_w, ssm_conv_b, ssm_a_log, ssm_dt_bias, ssm_d, ssm_norm_w, w_out):
    bsz, seqlen, d = x.shape
    tb = min(MIX_BLOCK, seqlen)
    assert seqlen % tb == 0 and tb & (tb - 1) == 0 and tb % (2 * LANES) == 0

    o_qkv, o_z, o_b, o_a = 0, 3 * GDN_W, 4 * GDN_W, 4 * GDN_W + GDN_HEADS
    o_sz = o_a + GDN_HEADS
    o_xbc = o_sz + SSM_W
    o_dt = o_xbc + SSM_W + SSM_BC_W
    pad = jnp.zeros((d, SMALL_W - 2 * GDN_HEADS - SSM_HEADS), w_in.dtype)
    w_in_r = jnp.concatenate([
        w_in[:, o_qkv:o_z], w_in[:, o_xbc:o_dt], w_in[:, o_z:o_b], w_in[:, o_sz:o_xbc],
        w_in[:, o_b:o_a], w_in[:, o_a:o_sz], w_in[:, o_dt:o_dt + SSM_HEADS], pad], axis=1).astype(BF16)

    head_params = jnp.zeros((8, LANES), F32)
    head_params = head_params.at[0, GDN_A_LANE:GDN_A_LANE + GDN_HEADS].set(gdn_dt_bias)
    head_params = head_params.at[0, SSM_DT_LANE:SSM_DT_LANE + SSM_HEADS].set(ssm_dt_bias)
    head_params = head_params.at[1, GDN_A_LANE:GDN_A_LANE + GDN_HEADS].set(gdn_a_log)
    head_params = head_params.at[1, SSM_DT_LANE:SSM_DT_LANE + SSM_HEADS].set(ssm_a_log)

    d_exp = jnp.repeat(ssm_d, SSM_HEAD_DIM).reshape(1, SSM_W)
    sel = np.zeros((LANES, SSM_W), np.float32)
    for hd in range(SSM_HEADS):
        sel[SSM_DT_LANE + hd, hd * SSM_HEAD_DIM:(hd + 1) * SSM_HEAD_DIM] = 1.0
    sel2 = jnp.asarray(np.concatenate([sel, sel], axis=0), BF16)
    eye, negm, tril, levels = _block_constants(tb)
    n_levels = levels.shape[0]

    scratch = [
        pltpu.VMEM((PRE_BUFFERS, TAIL_ROWS + tb, CONV_CHUNK), F32),
        pltpu.VMEM((TAIL_ROWS, CONV_W), F32),
        pltpu.VMEM((tb, 3 * GDN_W), F32),
        pltpu.VMEM((tb, SSM_W + SSM_BC_W), F32),
        pltpu.VMEM((tb, GATE_W), F32),
        pltpu.VMEM((tb, SSM_W), F32),
        pltpu.VMEM((tb, SSM_W), F32),
        pltpu.VMEM((tb, SSM_W), F32),
        pltpu.VMEM((SSM_GROUPS, tb, tb), F32),
        pltpu.VMEM((tb, GATE_W), BF16),
        pltpu.VMEM((GDN_HEADS, tb, tb), BF16),
        pltpu.VMEM((GDN_HEADS, tb, tb), BF16),
        pltpu.VMEM((GDN_HEADS, tb, tb), BF16),
        pltpu.VMEM((GDN_HEADS, tb, 2 * GDN_HEAD_DIM), BF16),
        pltpu.VMEM((GDN_HEADS, 2 * tb, GDN_HEAD_DIM), BF16),
        pltpu.VMEM((GDN_HEADS, tb, GDN_HEAD_DIM), BF16),
        pltpu.VMEM((GDN_HEADS, tb, GDN_HEAD_DIM), F32),
        pltpu.VMEM((GDN_HEADS, tb, GDN_HEAD_DIM), F32),
        pltpu.VMEM((GDN_HEADS, tb, GDN_HEAD_DIM), BF16),
        pltpu.VMEM((GDN_HEADS, GDN_HEAD_DIM, GDN_HEAD_DIM), F32),
        pltpu.VMEM((SSM_GROUPS, SSM_STATE, SSM_GROUP_W), F32),
    ]
    return pl.pallas_call(
        functools.partial(_mixer_kernel, tb=tb),
        grid=(bsz, seqlen // tb),
        in_specs=[
            pl.BlockSpec((1, tb, d), lambda b, j: (b, j, 0)),
            _resident((1, d)),
            _resident((d, IN_W_PADDED)),
            _resident((CONV_K, 3 * GDN_W)),
            _resident((CONV_K, SSM_W + SSM_BC_W)),
            _resident((1, SSM_W + SSM_BC_W)),
            _resident((8, LANES)),
            _resident((1, GDN_HEAD_DIM)),
            _resident((1, SSM_W)),
            _resident((1, SSM_W)),
            _resident((2 * LANES, SSM_W)),
            _resident((tb, tb)),
            _resident((tb, tb)),
            _resident((tb, tb)),
            _resident((n_levels, tb, tb)),
            _resident((GATE_W, d)),
        ],
        out_specs=pl.BlockSpec((1, tb, d), lambda b, j: (b, j, 0)),
        out_shape=jax.ShapeDtypeStruct((bsz, seqlen, d), F32),
        scratch_shapes=scratch,
        compiler_params=pltpu.CompilerParams(
            dimension_semantics=("arbitrary", "arbitrary"), vmem_limit_bytes=VMEM_LIMIT_BYTES),
    )(x, norm_w.reshape(1, d), w_in_r, gdn_conv_w, ssm_conv_w, ssm_conv_b.reshape(1, -1), head_params,
      gdn_norm_w.reshape(1, -1), d_exp, ssm_norm_w.reshape(1, -1), sel2,
      jnp.asarray(eye), jnp.asarray(negm), jnp.asarray(tril, BF16), jnp.asarray(levels, BF16),
      w_out.astype(BF16))


def kernel(x, ffn1_norm, ffn1_w_gate, ffn1_w_up, ffn1_w_down, mix_norm, w_in, gdn_conv_w, gdn_a_log,
           gdn_dt_bias, gdn_norm_w, ssm_conv_w, ssm_conv_b, ssm_a_log, ssm_dt_bias, ssm_d, ssm_norm_w,
           w_out, ffn2_norm, ffn2_w_gate, ffn2_w_up, ffn2_w_down, final_norm):
    bsz, seqlen, d = x.shape
    depth = ffn1_norm.shape[0]
    for i in range(depth):
        x2 = _ffn(x.reshape(bsz * seqlen, d), ffn1_norm[i], ffn1_w_gate[i], ffn1_w_up[i],
                  ffn1_w_down[i], final_norm, False)
        x = _mixer(x2.reshape(bsz, seqlen, d), mix_norm[i], w_in[i], gdn_conv_w[i], gdn_a_log[i],
                   gdn_dt_bias[i], gdn_norm_w[i], ssm_conv_w[i], ssm_conv_b[i], ssm_a_log[i],
                   ssm_dt_bias[i], ssm_d[i], ssm_norm_w[i], w_out[i])
        x2 = _ffn(x.reshape(bsz * seqlen, d), ffn2_norm[i], ffn2_w_gate[i], ffn2_w_up[i],
                  ffn2_w_down[i], final_norm, i == depth - 1)
        x = x2.reshape(bsz, seqlen, d)
    return x
```

```python
import functools

import jax
import jax.numpy as jnp
import numpy as np
from jax import lax
from jax.experimental import pallas as pl
from jax.experimental.pallas import tpu as pltpu

F32 = jnp.float32
BF16 = jnp.bfloat16

D_MODEL = 1024
D_FF = 2816
CONV_K = 4
EPS = 1e-6
GDN_HEADS = 8
GDN_HEAD_DIM = 128
GDN_W = GDN_HEADS * GDN_HEAD_DIM
SSM_HEADS = 16
SSM_HEAD_DIM = 64
SSM_GROUPS = 2
SSM_STATE = 128
SSM_W = SSM_HEADS * SSM_HEAD_DIM
SSM_GROUP_W = SSM_W // SSM_GROUPS
SSM_BC_W = 2 * SSM_GROUPS * SSM_STATE

LANES = 128
CONV_W = 3 * GDN_W + SSM_W + SSM_BC_W
GATE_W = GDN_W + SSM_W
SMALL_W = LANES
IN_W_PADDED = CONV_W + GATE_W + SMALL_W
XBC_OFF = 3 * GDN_W
BETA_LANE = 0
GDN_A_LANE = GDN_HEADS
SSM_DT_LANE = 2 * GDN_HEADS
TAIL_ROWS = 8

FFN_TILE = 1024
FFN_SUBTILE = 512
CONV_CHUNK = 512
PRE_BUFFERS = 2
MIX_BLOCK = 256
BLOCKWISE_MIN = 32
VMEM_LIMIT_BYTES = 56 * 1024 * 1024

LOG2E = 1.4426950408889634


def _rms(x, w):
    ms = jnp.mean(x * x, axis=-1, keepdims=True)
    return x * lax.rsqrt(ms + EPS) * w


def _silu(x):
    return x * jax.nn.sigmoid(x)


def _softplus(x):
    return jnp.maximum(x, 0.0) + jnp.log1p(jnp.exp(-jnp.abs(x)))


def _dot(a, b):
    return jnp.dot(a, b, preferred_element_type=F32)


def _dot_nt(a, b):
    return lax.dot_general(a, b, (((1,), (1,)), ((), ())), preferred_element_type=F32)


def _dot_tn(a, b):
    return lax.dot_general(a, b, (((0,), (0,)), ((), ())), preferred_element_type=F32)


def _ffn_kernel(x_ref, nw_ref, wg_ref, wu_ref, wd_ref, fw_ref, o_ref, *, final_norm):
    for r in range(0, x_ref.shape[0], FFN_SUBTILE):
        rows = slice(r, r + FFN_SUBTILE)
        x = x_ref[rows, :]
        h = _rms(x, nw_ref[...]).astype(BF16)
        a = _dot(h, wg_ref[...])
        b = _dot(h, wu_ref[...])
        g = (_silu(a) * b).astype(BF16)
        out = x + 0.5 * _dot(g, wd_ref[...])
        if final_norm:
            out = _rms(out, fw_ref[...])
        o_ref[rows, :] = out


def _resident(shape):
    return pl.BlockSpec(shape, lambda *_: (0,) * len(shape), pipeline_mode=pl.Buffered(1))


def _ffn(x2d, norm_w, w_gate, w_up, w_down, final_w, final_norm):
    t, d = x2d.shape
    tm = min(FFN_TILE, t)
    assert t % tm == 0 and tm % FFN_SUBTILE == 0
    return pl.pallas_call(
        functools.partial(_ffn_kernel, final_norm=final_norm),
        grid=(t // tm,),
        in_specs=[
            pl.BlockSpec((tm, d), lambda i: (i, 0)),
            _resident((1, d)),
            _resident((d, D_FF)),
            _resident((d, D_FF)),
            _resident((D_FF, d)),
            _resident((1, d)),
        ],
        out_specs=pl.BlockSpec((tm, d), lambda i: (i, 0)),
        out_shape=jax.ShapeDtypeStruct((t, d), F32),
        compiler_params=pltpu.CompilerParams(
            dimension_semantics=("arbitrary",), vmem_limit_bytes=VMEM_LIMIT_BYTES),
    )(x2d, norm_w.reshape(1, d), w_gate.astype(BF16), w_up.astype(BF16), w_down.astype(BF16),
      final_w.reshape(1, d))


def _block_constants(tb):
    r = np.arange(tb)[:, None]
    c = np.arange(tb)[None, :]
    eye = (r == c).astype(np.float32)
    tril = (c <= r).astype(np.float32)
    negm = np.where(c <= r, 0.0, -np.inf).astype(np.float32)
    levels = [((r // 2 == c // 2) & (r > c)).astype(np.float32)]
    b = 2
    while b < BLOCKWISE_MIN:
        levels.append(-((r // (2 * b) == c // (2 * b)) & (r // b != c // b) & (r > c)).astype(np.float32))
        b *= 2
    return eye, negm, tril, np.stack(levels)


def _mixer_kernel(x_ref, nw_ref, win_ref, gcw_ref, scw_ref, scb_ref, hp_ref, gnw_ref, dexp_ref,
                  snw_ref, sel_ref, eye_ref, negm_ref, tril_ref, lvl_ref, wout_ref, o_ref,
                  pre, tail, qkv, xbc, gates, xdt_s, xe_s, dine_s, cb_s, mix_s,
                  a_s, t_s, qk_s, rhs_s, lhs_s, kd_s, u_s, oq_s, vn_s, sg, ss, *, tb):
    n_levels = lvl_ref.shape[0]
    n_conv_chunks = CONV_W // CONV_CHUNK

    @pl.when(pl.program_id(1) == 0)
    def _():
        sg[...] = jnp.zeros_like(sg)
        ss[...] = jnp.zeros_like(ss)
        tail[...] = jnp.zeros_like(tail)

    x = x_ref[0]
    h = _rms(x, nw_ref[...]).astype(BF16)

    def expand(v):
        hi = v.astype(BF16)
        lo_ = (v - hi.astype(F32)).astype(BF16)
        return _dot(jnp.concatenate([hi, lo_], axis=1), sel_ref[...])

    for j in range(n_conv_chunks):
        lo = j * CONV_CHUNK
        buf = pre.at[j % PRE_BUFFERS]
        buf[0:TAIL_ROWS, :] = tail[:, lo:lo + CONV_CHUNK]
        buf[TAIL_ROWS:TAIL_ROWS + tb, :] = _dot(h, win_ref[:, lo:lo + CONV_CHUNK])
        if lo < XBC_OFF:
            w_ref, wlo, bias = gcw_ref, lo, None
        else:
            w_ref, wlo = scw_ref, lo - XBC_OFF
            bias = scb_ref[:, wlo:wlo + CONV_CHUNK]
        w0, w1, w2, w3 = (w_ref[k:k + 1, wlo:wlo + CONV_CHUNK] for k in range(CONV_K))
        p_all = buf[...]
        p_m2 = pltpu.roll(p_all, 2, 0)
        odd = pltpu.roll(w2 * p_all + w0 * p_m2, 1, 0)
        acc = (w3 * p_all[TAIL_ROWS:] + w1 * p_m2[TAIL_ROWS:]) + odd[TAIL_ROWS:]
        if bias is not None:
            acc = acc + bias
        y = _silu(acc)
        if lo < XBC_OFF:
            qkv[:, lo:lo + CONV_CHUNK] = y
        else:
            xbc[:, wlo:wlo + CONV_CHUNK] = y
        tail[:, lo:lo + CONV_CHUNK] = buf[tb:tb + TAIL_ROWS, :]
        if j == 1:
            small = _dot(h, win_ref[:, CONV_W + GATE_W:IN_W_PADDED])
            sp = _softplus(small + hp_ref[0:1, :])
            sig = jax.nn.sigmoid(small)
            logdec = -jnp.exp(hp_ref[1:2, :]) * sp

            ld_hi = logdec.astype(BF16)
            ld_r = logdec - ld_hi.astype(F32)
            ld_mid = ld_r.astype(BF16)
            ld_lo = (ld_r - ld_mid.astype(F32)).astype(BF16)
            cs3 = _dot(tril_ref[...], jnp.concatenate([ld_hi, ld_mid, ld_lo], axis=1))
            cs = (cs3[:, 0:LANES] + cs3[:, LANES:2 * LANES]) + cs3[:, 2 * LANES:3 * LANES]
            din = jnp.exp(cs)
            toe = jnp.exp(cs[tb - 1:tb, :] - cs)
            din_last = din[tb - 1:tb, :]
            cs_l2 = cs * LOG2E
            cs_l2t = cs_l2.T

    half = lax.broadcasted_iota(jnp.int32, (tb, LANES), 1) // SSM_HEAD_DIM
    pairs_per_group = SSM_GROUP_W // LANES

    def ssd_pair(pi):
        g, pp = divmod(pi, pairs_per_group)
        csl = slice(g * SSM_GROUP_W + pp * LANES, g * SSM_GROUP_W + (pp + 1) * LANES)
        xpair = xdt_s[:, csl]
        acc = dine_s[:, csl]
        for tt in range(LANES // SSM_HEAD_DIM):
            hl = SSM_DT_LANE + g * (SSM_HEADS // SSM_GROUPS) + pp * (LANES // SSM_HEAD_DIM) + tt
            seg = jnp.exp2(cs_l2[:, hl:hl + 1] - cs_l2t[hl:hl + 1, :] + negm_ref[...])
            acc = acc + _dot((cb_s[g] * seg).astype(BF16), jnp.where(half == tt, xpair, 0.0).astype(BF16))
        y = acc + xbc[:, csl] * dexp_ref[:, csl]
        gsl2 = slice(GDN_W + g * SSM_GROUP_W + pp * LANES, GDN_W + g * SSM_GROUP_W + (pp + 1) * LANES)
        xe_s[:, csl] = y * _silu(gates[:, gsl2])

    eye_b = eye_ref[...].astype(BF16)
    gate_piece = GATE_W // GDN_HEADS
    for hd in range(GDN_HEADS):
        glo = hd * gate_piece
        gates[:, glo:glo + gate_piece] = _dot(h, win_ref[:, CONV_W + glo:CONV_W + glo + gate_piece])
        sl = slice(hd * GDN_HEAD_DIM, (hd + 1) * GDN_HEAD_DIM)
        q = qkv[:, sl]
        k = qkv[:, GDN_W + hd * GDN_HEAD_DIM:GDN_W + (hd + 1) * GDN_HEAD_DIM]
        v = qkv[:, 2 * GDN_W + hd * GDN_HEAD_DIM:2 * GDN_W + (hd + 1) * GDN_HEAD_DIM]
        qn = q * (lax.rsqrt(jnp.sum(q * q, axis=-1, keepdims=True) + EPS) * GDN_HEAD_DIM ** -0.5)
        kn = k * lax.rsqrt(jnp.sum(k * k, axis=-1, keepdims=True) + EPS)
        gl = GDN_A_LANE + hd
        beta = sig[:, BETA_LANE + hd:BETA_LANE + hd + 1]
        decay = jnp.exp2(cs_l2[:, gl:gl + 1] - cs_l2t[gl:gl + 1, :] + negm_ref[...])
        knb = kn.astype(BF16)
        kq = _dot_nt(jnp.concatenate([knb, qn.astype(BF16)], axis=0), knb)
        a_b = (beta * kq[0:tb] * decay).astype(BF16)
        a_s[hd] = a_b
        t_s[hd] = eye_b - a_b * lvl_ref[0]
        qk_s[hd] = (kq[tb:2 * tb] * decay).astype(BF16)
        qkv[:, sl] = qn
        qkv[:, GDN_W + hd * GDN_HEAD_DIM:GDN_W + (hd + 1) * GDN_HEAD_DIM] = kn

    def gdn_operands(hd):
        sl = slice(hd * GDN_HEAD_DIM, (hd + 1) * GDN_HEAD_DIM)
        qn = qkv[:, sl]
        kn = qkv[:, GDN_W + hd * GDN_HEAD_DIM:GDN_W + (hd + 1) * GDN_HEAD_DIM]
        v = qkv[:, 2 * GDN_W + hd * GDN_HEAD_DIM:2 * GDN_W + (hd + 1) * GDN_HEAD_DIM]
        gl = GDN_A_LANE + hd
        beta = sig[:, BETA_LANE + hd:BETA_LANE + hd + 1]
        eg = din[:, gl:gl + 1]
        rhs_s[hd] = jnp.concatenate([v * beta, kn * (beta * eg)], axis=1).astype(BF16)
        lhs_s[hd, tb:2 * tb, :] = (qn * eg).astype(BF16)
        kd_s[hd] = (kn * toe[:, gl:gl + 1]).astype(BF16)

    xdt = xbc[:, 0:SSM_W] * expand(sp)
    xdt_s[...] = xdt
    xe_s[...] = xdt * expand(toe)
    dine_s[...] = expand(din)
    for g in range(SSM_GROUPS):
        gsl = slice(g * SSM_GROUP_W, (g + 1) * SSM_GROUP_W)
        bg = xbc[:, SSM_W + g * SSM_STATE:SSM_W + (g + 1) * SSM_STATE].astype(BF16)
        cg = xbc[:, SSM_W + (SSM_GROUPS + g) * SSM_STATE:
                 SSM_W + (SSM_GROUPS + g + 1) * SSM_STATE].astype(BF16)
        cb_s[g] = _dot_nt(cg, bg)
        state = ss[g]
        ss[g] = state * dine_s[tb - 1:tb, gsl] + _dot_tn(bg, xe_s[:, gsl].astype(BF16))
        dine_s[:, gsl] = _dot(cg, state.astype(BF16)) * dine_s[:, gsl]

    n_pairs = SSM_GROUPS * pairs_per_group
    top_stages = 2 * ((tb // BLOCKWISE_MIN).bit_length() - 1)
    pairs_left = list(range(n_pairs))
    heads_per_level = -(-GDN_HEADS // (n_levels - 1))

    def some_pairs(n):
        for _ in range(min(n, len(pairs_left))):
            ssd_pair(pairs_left.pop(0))

    for lv in range(1, n_levels):
        for hd in range(GDN_HEADS):
            for i in range(tb // LANES):
                tile = slice(i * LANES, (i + 1) * LANES)
                t_b = t_s[hd, tile, tile]
                xneg = _dot(a_s[hd, tile, tile] * lvl_ref[lv, tile, tile], t_b)
                t_s[hd, tile, tile] = t_b + _dot(t_b, xneg.astype(BF16)).astype(BF16)
        if n_levels - lv <= n_pairs - top_stages:
            some_pairs(1)
        for hd in range((lv - 1) * heads_per_level, min(lv * heads_per_level, GDN_HEADS)):
            gdn_operands(hd)

    def on_k_axis(blk, at):
        parts = []
        if at:
            parts.append(jnp.zeros((at, LANES), BF16))
        parts.append(blk)
        if at + blk.shape[0] < LANES:
            parts.append(jnp.zeros((LANES - at - blk.shape[0], LANES), BF16))
        return jnp.concatenate(parts, axis=0)

    b = BLOCKWISE_MIN
    while b < LANES:
        blocks = [(hd, r0, r0 % LANES) for hd in range(GDN_HEADS) for r0 in range(0, tb, 2 * b)]
        x_blks = []
        for hd, r0, c0 in blocks:
            tile = slice(r0 - c0, r0 - c0 + LANES)
            x_blks.append(_dot(a_s[hd, r0 + b:r0 + 2 * b, tile], on_k_axis(t_s[hd, r0:r0 + b, tile], c0)))
        some_pairs(1)
        for (hd, r0, c0), x_blk in zip(blocks, x_blks):
            tile = slice(r0 - c0, r0 - c0 + LANES)
            t_bot = t_s[hd, r0 + b:r0 + 2 * b, tile]
            p_blk = _dot(t_bot, on_k_axis(x_blk.astype(BF16), c0 + b))
            t_s[hd, r0 + b:r0 + 2 * b, tile] = t_bot - p_blk.astype(BF16)
        some_pairs(1)
        b *= 2
    blocks = [(hd, i) for hd in range(GDN_HEADS) for i in range(tb // (2 * LANES))]

    def rows128(i):
        return slice(2 * i * LANES, (2 * i + 1) * LANES), slice((2 * i + 1) * LANES, (2 * i + 2) * LANES)

    x_blks = []
    for hd, i in blocks:
        top, bot = rows128(i)
        x_blks.append(_dot(a_s[hd, bot, top], t_s[hd, top, top]))
    some_pairs(1)
    for (hd, i), x_blk in zip(blocks, x_blks):
        top, bot = rows128(i)
        t_s[hd, bot, top] = (-_dot(t_s[hd, bot, bot], x_blk.astype(BF16))).astype(BF16)
    some_pairs(n_pairs)

    for g in range(SSM_GROUPS):
        gsl = slice(g * SSM_GROUP_W, (g + 1) * SSM_GROUP_W)
        mix_s[:, GDN_W + g * SSM_GROUP_W:GDN_W + (g + 1) * SSM_GROUP_W] = (
            _rms(xe_s[:, gsl], snw_ref[:, gsl]).astype(BF16))
    o_ref[0] = x + _dot(mix_s[:, GDN_W:GATE_W], wout_ref[GDN_W:GATE_W, :])

    for hd in range(GDN_HEADS):
        uw = _dot(t_s[hd], rhs_s[hd])
        u_s[hd] = uw[:, 0:GDN_HEAD_DIM]
        lhs_s[hd, 0:tb, :] = uw[:, GDN_HEAD_DIM:2 * GDN_HEAD_DIM].astype(BF16)

    for hd in range(GDN_HEADS):
        wq = _dot(lhs_s[hd], sg[hd].astype(BF16))
        vn_s[hd] = (u_s[hd] - wq[0:tb]).astype(BF16)
        oq_s[hd] = wq[tb:2 * tb]

    for hd in range(GDN_HEADS):
        sl = slice(hd * GDN_HEAD_DIM, (hd + 1) * GDN_HEAD_DIM)
        gl = GDN_A_LANE + hd
        v_new = vn_s[hd]
        o = oq_s[hd] + _dot(qk_s[hd], v_new)
        sg[hd] = sg[hd] * din_last[:, gl:gl + 1] + _dot_tn(kd_s[hd], v_new)
        mix_s[:, sl] = (_rms(o, gnw_ref[...]) * _silu(gates[:, sl])).astype(BF16)

    o_ref[0] += _dot(mix_s[:, 0:GDN_W], wout_ref[0:GDN_W, :])


def _mixer(x, norm_w, w_in, gdn_conv_w, gdn_a_log, gdn_dt_bias, gdn_norm_w,
           ssm_conv_w, ssm_conv_b, ssm_a_log, ssm_dt_bias, ssm_d, ssm_norm_w, w_out):
    bsz, seqlen, d = x.shape
    tb = min(MIX_BLOCK, seqlen)
    assert seqlen % tb == 0 and tb & (tb - 1) == 0 and tb % (2 * LANES) == 0

    o_qkv, o_z, o_b, o_a = 0, 3 * GDN_W, 4 * GDN_W, 4 * GDN_W + GDN_HEADS
    o_sz = o_a + GDN_HEADS
    o_xbc = o_sz + SSM_W
    o_dt = o_xbc + SSM_W + SSM_BC_W
    pad = jnp.zeros((d, SMALL_W - 2 * GDN_HEADS - SSM_HEADS), w_in.dtype)
    w_in_r = jnp.concatenate([
        w_in[:, o_qkv:o_z], w_in[:, o_xbc:o_dt], w_in[:, o_z:o_b], w_in[:, o_sz:o_xbc],
        w_in[:, o_b:o_a], w_in[:, o_a:o_sz], w_in[:, o_dt:o_dt + SSM_HEADS], pad], axis=1).astype(BF16)

    head_params = jnp.zeros((8, LANES), F32)
    head_params = head_params.at[0, GDN_A_LANE:GDN_A_LANE + GDN_HEADS].set(gdn_dt_bias)
    head_params = head_params.at[0, SSM_DT_LANE:SSM_DT_LANE + SSM_HEADS].set(ssm_dt_bias)
    head_params = head_params.at[1, GDN_A_LANE:GDN_A_LANE + GDN_HEADS].set(gdn_a_log)
    head_params = head_params.at[1, SSM_DT_LANE:SSM_DT_LANE + SSM_HEADS].set(ssm_a_log)

    d_exp = jnp.repeat(ssm_d, SSM_HEAD_DIM).reshape(1, SSM_W)
    sel = np.zeros((LANES, SSM_W), np.float32)
    for hd in range(SSM_HEADS):
        sel[SSM_DT_LANE + hd, hd * SSM_HEAD_DIM:(hd + 1) * SSM_HEAD_DIM] = 1.0
    sel2 = jnp.asarray(np.concatenate([sel, sel], axis=0), BF16)
    eye, negm, tril, levels = _block_constants(tb)
    n_levels = levels.shape[0]

    scratch = [
        pltpu.VMEM((PRE_BUFFERS, TAIL_ROWS + tb, CONV_CHUNK), F32),
        pltpu.VMEM((TAIL_ROWS, CONV_W), F32),
        pltpu.VMEM((tb, 3 * GDN_W), F32),
        pltpu.VMEM((tb, SSM_W + SSM_BC_W), F32),
        pltpu.VMEM((tb, GATE_W), F32),
        pltpu.VMEM((tb, SSM_W), F32),
        pltpu.VMEM((tb, SSM_W), F32),
        pltpu.VMEM((tb, SSM_W), F32),
        pltpu.VMEM((SSM_GROUPS, tb, tb), F32),
        pltpu.VMEM((tb, GATE_W), BF16),
        pltpu.VMEM((GDN_HEADS, tb, tb), BF16),
        pltpu.VMEM((GDN_HEADS, tb, tb), BF16),
        pltpu.VMEM((GDN_HEADS, tb, tb), BF16),
        pltpu.VMEM((GDN_HEADS, tb, 2 * GDN_HEAD_DIM), BF16),
        pltpu.VMEM((GDN_HEADS, 2 * tb, GDN_HEAD_DIM), BF16),
        pltpu.VMEM((GDN_HEADS, tb, GDN_HEAD_DIM), BF16),
        pltpu.VMEM((GDN_HEADS, tb, GDN_HEAD_DIM), F32),
        pltpu.VMEM((GDN_HEADS, tb, GDN_HEAD_DIM), F32),
        pltpu.VMEM((GDN_HEADS, tb, GDN_HEAD_DIM), BF16),
        pltpu.VMEM((GDN_HEADS, GDN_HEAD_DIM, GDN_HEAD_DIM), F32),
        pltpu.VMEM((SSM_GROUPS, SSM_STATE, SSM_GROUP_W), F32),
    ]
    return pl.pallas_call(
        functools.partial(_mixer_kernel, tb=tb),
        grid=(bsz, seqlen // tb),
        in_specs=[
            pl.BlockSpec((1, tb, d), lambda b, j: (b, j, 0)),
            _resident((1, d)),
            _resident((d, IN_W_PADDED)),
            _resident((CONV_K, 3 * GDN_W)),
            _resident((CONV_K, SSM_W + SSM_BC_W)),
            _resident((1, SSM_W + SSM_BC_W)),
            _resident((8, LANES)),
            _resident((1, GDN_HEAD_DIM)),
            _resident((1, SSM_W)),
            _resident((1, SSM_W)),
            _resident((2 * LANES, SSM_W)),
            _resident((tb, tb)),
            _resident((tb, tb)),
            _resident((tb, tb)),
            _resident((n_levels, tb, tb)),
            _resident((GATE_W, d)),
        ],
        out_specs=pl.BlockSpec((1, tb, d), lambda b, j: (b, j, 0)),
        out_shape=jax.ShapeDtypeStruct((bsz, seqlen, d), F32),
        scratch_shapes=scratch,
        compiler_params=pltpu.CompilerParams(
            dimension_semantics=("arbitrary", "arbitrary"), vmem_limit_bytes=VMEM_LIMIT_BYTES),
    )(x, norm_w.reshape(1, d), w_in_r, gdn_conv_w, ssm_conv_w, ssm_conv_b.reshape(1, -1), head_params,
      gdn_norm_w.reshape(1, -1), d_exp, ssm_norm_w.reshape(1, -1), sel2,
      jnp.asarray(eye), jnp.asarray(negm), jnp.asarray(tril, BF16), jnp.asarray(levels, BF16),
      w_out.astype(BF16))


def kernel(x, ffn1_norm, ffn1_w_gate, ffn1_w_up, ffn1_w_down, mix_norm, w_in, gdn_conv_w, gdn_a_log,
           gdn_dt_bias, gdn_norm_w, ssm_conv_w, ssm_conv_b, ssm_a_log, ssm_dt_bias, ssm_d, ssm_norm_w,
           w_out, ffn2_norm, ffn2_w_gate, ffn2_w_up, ffn2_w_down, final_norm):
    bsz, seqlen, d = x.shape
    depth = ffn1_norm.shape[0]
    for i in range(depth):
        x2 = _ffn(x.reshape(bsz * seqlen, d), ffn1_norm[i], ffn1_w_gate[i], ffn1_w_up[i],
                  ffn1_w_down[i], final_norm, False)
        x = _mixer(x2.reshape(bsz, seqlen, d), mix_norm[i], w_in[i], gdn_conv_w[i], gdn_a_log[i],
                   gdn_dt_bias[i], gdn_norm_w[i], ssm_conv_w[i], ssm_conv_b[i], ssm_a_log[i],
                   ssm_dt_bias[i], ssm_d[i], ssm_norm_w[i], w_out[i])
        x2 = _ffn(x.reshape(bsz * seqlen, d), ffn2_norm[i], ffn2_w_gate[i], ffn2_w_up[i],
                  ffn2_w_down[i], final_norm, i == depth - 1)
        x = x2.reshape(bsz, seqlen, d)
    return x
```
